```python
import functools
import jax, jax.numpy as jnp
from jax import lax
import numpy as np

D_MODEL = 1024
BATCH = 16
SEQ = 2048
DEPTH = 1
DEC_BATCH = 32
DEC_SEQ = 8
PAST_LEN = 16384
PAGE_SIZE = 128

HEAD_DIM = 64
N_HEADS_A = D_MODEL // 128
WIDTH_A = N_HEADS_A * HEAD_DIM
WIDTH_B = D_MODEL - WIDTH_A
MIX_WIDTH = WIDTH_A + WIDTH_B
DILATED_BRANCHES = ((128, 1), (512, 4), (2048, 16))
WIN_MAX = 2048
CONV_WIDTH = 31
D_FF = 2816
N_MEM = 256
N_HEADS_X = 4
HEAD_DIM_X = D_MODEL // N_HEADS_X
EPS = 1e-6

kernel_name = 'hybrid_dilated_conformer_decoder_step'


def rmsnorm(x, g):
    xf = x.astype(jnp.float32)
    y = xf * lax.rsqrt(jnp.mean(xf * xf, axis=-1, keepdims=True) + EPS)
    return (y * g.astype(jnp.float32)).astype(x.dtype)


def layernorm(x, g, b):
    xf = x.astype(jnp.float32)
    mu = jnp.mean(xf, axis=-1, keepdims=True)
    xc = xf - mu
    y = xc * lax.rsqrt(jnp.mean(xc * xc, axis=-1, keepdims=True) + EPS)
    return (y * g.astype(jnp.float32) + b.astype(jnp.float32)).astype(x.dtype)


def ffn_half_step(x, g, wg, wu, wd):
    h = rmsnorm(x, g)
    return x + 0.5 * ((jax.nn.silu(h @ wg) * (h @ wu)) @ wd)


def alibi_slopes(n):
    return 2.0 ** (-8.0 * jnp.arange(1, n + 1, dtype=jnp.float32) / n)


def project_in(h, w_in):
    B, T, _ = h.shape
    z = h @ w_in
    q, k, v, a, g = jnp.split(z, [WIDTH_A, 2 * WIDTH_A, 3 * WIDTH_A, 3 * WIDTH_A + WIDTH_B], axis=-1)
    heads = lambda t: t.reshape(B, T, N_HEADS_A, HEAD_DIM)
    return heads(q), heads(k), heads(v), a * jax.nn.sigmoid(g)


def conv_branch(u_ext, conv_w, conv_b, ln_g, ln_b):
    y = lax.conv_general_dilated(u_ext, conv_w[:, None, :], window_strides=(1,), padding='VALID',
                                 dimension_numbers=('NWC', 'WIO', 'NWC'),
                                 feature_group_count=u_ext.shape[-1])
    return jax.nn.silu(layernorm(y + conv_b, ln_g, ln_b))


def dilated_branch_prompt(q, k, v, slopes, window, dilation):
    B, S, H, Dh = q.shape
    w = window // dilation
    L = S // dilation
    nblk = -(-L // w)
    Lp = nblk * w
    split = lambda a: a.reshape(B, L, dilation, H, Dh)
    qb = jnp.pad(split(q), ((0, 0), (0, Lp - L), (0, 0), (0, 0), (0, 0))).reshape(B, nblk, w, dilation, H, Dh)
    kpad = ((0, 0), (w, Lp - L), (0, 0), (0, 0), (0, 0))
    ks = jnp.pad(split(k), kpad).reshape(B, nblk + 1, w, dilation, H, Dh)
    vs = jnp.pad(split(v), kpad).reshape(B, nblk + 1, w, dilation, H, Dh)
    kb = jnp.concatenate([ks[:, :-1], ks[:, 1:]], axis=2)
    vb = jnp.concatenate([vs[:, :-1], vs[:, 1:]], axis=2)
    s = jnp.einsum('bnqrhe,bnkrhe->bnrhqk', qb, kb).astype(jnp.float32) * (Dh ** -0.5)
    qi = jnp.arange(w)[:, None]
    kj = jnp.arange(2 * w)[None, :]
    dist = qi + w - kj
    blk = jnp.arange(nblk)[:, None, None]
    valid = (dist >= 0) & (dist <= w) & (blk * w + kj - w >= 0)
    bias = -slopes[:, None, None] * (dist * dilation).astype(jnp.float32)
    s = jnp.where(valid[None, :, None, None], s + bias, -jnp.inf)
    m = jnp.max(s, axis=-1)
    p = jnp.exp(s - m[..., None])
    den = jnp.sum(p, axis=-1)
    num = jnp.einsum('bnrhqk,bnkrhe->bnqrhe', p, vb.astype(jnp.float32))
    to_seq = lambda t: t.transpose(0, 1, 4, 2, 3).reshape(B, Lp, dilation, H)[:, :L].reshape(B, S, H)
    num = num.reshape(B, Lp, dilation, H, Dh)[:, :L].reshape(B, S, H, Dh)
    return to_seq(m), to_seq(den), num


def dilated_branch_sample(q, k_all, v_all, slopes, window, dilation):
    B, T, H, Dh = q.shape
    u0 = k_all.shape[1] - T
    w = window // dilation
    dist = jnp.arange(w + 1) * dilation
    u = u0 + jnp.arange(T)[:, None] - dist[None, :]
    valid = u >= 0
    uc = jnp.maximum(u, 0)
    kg = k_all[:, uc]
    vg = v_all[:, uc]
    s = jnp.einsum('bthe,btjhe->bthj', q, kg).astype(jnp.float32) * (Dh ** -0.5)
    s = s - slopes[:, None] * dist.astype(jnp.float32)[None, :]
    s = jnp.where(valid[None, :, None, :], s, -jnp.inf)
    m = jnp.max(s, axis=-1)
    p = jnp.exp(s - m[..., None])
    den = jnp.sum(p, axis=-1)
    num = jnp.einsum('bthj,btjhe->bthe', p, vg.astype(jnp.float32))
    return m, den, num


def merge_by_denominator(parts):
    m = functools.reduce(jnp.maximum, [pt[0] for pt in parts])
    scales = [jnp.exp(pt[0] - m) for pt in parts]
    den = functools.reduce(jnp.add, [sc * pt[1] for sc, pt in zip(scales, parts)])
    num = functools.reduce(jnp.add, [sc[..., None] * pt[2] for sc, pt in zip(scales, parts)])
    return num / den[..., None]


def mem_kv(mem, g, w_ck, w_cv):
    B = mem.shape[0]
    mn = rmsnorm(mem, g)
    return ((mn @ w_ck).reshape(B, N_MEM, N_HEADS_X, HEAD_DIM_X),
            (mn @ w_cv).reshape(B, N_MEM, N_HEADS_X, HEAD_DIM_X))


def cross_attend(h, mk, mv, w_cq, w_co):
    B, T, _ = h.shape
    q = (h @ w_cq).reshape(B, T, N_HEADS_X, HEAD_DIM_X)
    s = jnp.einsum('bthe,bmhe->bhtm', q, mk).astype(jnp.float32) * (HEAD_DIM_X ** -0.5)
    p = jax.nn.softmax(s, axis=-1)
    o = jnp.einsum('bhtm,bmhe->bthe', p, mv.astype(jnp.float32)).reshape(B, T, N_HEADS_X * HEAD_DIM_X)
    return o.astype(h.dtype) @ w_co


def setup_inputs(seed: int = 0) -> dict:
    key = jax.random.key(seed)
    ks = iter(jax.random.split(key, 48))
    nrm = lambda shape, scale: jax.random.normal(next(ks), shape, jnp.float32) * scale
    gain = lambda shape: 1.0 + nrm(shape, 0.02)
    w_buf = min(WIN_MAX, PAST_LEN)
    L = DEPTH
    return {
        'x_prompt': nrm((BATCH, SEQ, D_MODEL), 1.0),
        'x_sample': nrm((DEC_BATCH, DEC_SEQ, D_MODEL), 1.0),
        'mem_prompt': nrm((BATCH, N_MEM, D_MODEL), 1.0),
        'cache_win_k': nrm((L, DEC_BATCH, w_buf, N_HEADS_A, HEAD_DIM), 1.0),
        'cache_win_v': nrm((L, DEC_BATCH, w_buf, N_HEADS_A, HEAD_DIM), 1.0),
        'cache_conv': nrm((L, DEC_BATCH, CONV_WIDTH - 1, WIDTH_B), 0.5),
        'cache_mem_k': nrm((L, DEC_BATCH, N_MEM, N_HEADS_X, HEAD_DIM_X), 1.0),
        'cache_mem_v': nrm((L, DEC_BATCH, N_MEM, N_HEADS_X, HEAD_DIM_X), 1.0),
        'ffn1_norm': gain((L, D_MODEL)),
        'ffn1_gate': nrm((L, D_MODEL, D_FF), D_MODEL ** -0.5),
        'ffn1_up': nrm((L, D_MODEL, D_FF), D_MODEL ** -0.5),
        'ffn1_down': nrm((L, D_FF, D_MODEL), D_FF ** -0.5),
        'mix_norm': gain((L, D_MODEL)),
        'w_in': nrm((L, D_MODEL, 3 * WIDTH_A + 2 * WIDTH_B), D_MODEL ** -0.5),
        'conv_w': nrm((L, CONV_WIDTH, WIDTH_B), CONV_WIDTH ** -0.5),
        'conv_b': nrm((L, WIDTH_B), 0.02),
        'conv_ln_g': gain((L, WIDTH_B)),
        'conv_ln_b': nrm((L, WIDTH_B), 0.02),
        'w_out': nrm((L, MIX_WIDTH, D_MODEL), MIX_WIDTH ** -0.5),
        'xattn_norm': gain((L, D_MODEL)),
        'mem_norm': gain((L, D_MODEL)),
        'w_cq': nrm((L, D_MODEL, N_HEADS_X * HEAD_DIM_X), D_MODEL ** -0.5),
        'w_ck': nrm((L, D_MODEL, N_HEADS_X * HEAD_DIM_X), D_MODEL ** -0.5),
        'w_cv': nrm((L, D_MODEL, N_HEADS_X * HEAD_DIM_X), D_MODEL ** -0.5),
        'w_co': nrm((L, N_HEADS_X * HEAD_DIM_X, D_MODEL), (N_HEADS_X * HEAD_DIM_X) ** -0.5),
        'ffn2_norm': gain((L, D_MODEL)),
        'ffn2_gate': nrm((L, D_MODEL, D_FF), D_MODEL ** -0.5),
        'ffn2_up': nrm((L, D_MODEL, D_FF), D_MODEL ** -0.5),
        'ffn2_down': nrm((L, D_FF, D_MODEL), D_FF ** -0.5),
        'final_norm': gain((D_MODEL,)),
    }


def reference(x_prompt, x_sample, mem_prompt, cache_win_k, cache_win_v, cache_conv, cache_mem_k, cache_mem_v,
              ffn1_norm, ffn1_gate, ffn1_up, ffn1_down, mix_norm, w_in, conv_w, conv_b, conv_ln_g, conv_ln_b,
              w_out, xattn_norm, mem_norm, w_cq, w_ck, w_cv, w_co, ffn2_norm, ffn2_gate, ffn2_up, ffn2_down,
              final_norm):
    slopes = alibi_slopes(N_HEADS_A)
    yp, ys = x_prompt, x_sample
    Bp, S, _ = x_prompt.shape
    Bs, T, _ = x_sample.shape
    keep = min(WIN_MAX, S)
    pk, pv, pc, pmk, pmv, sk, sv, sc = [], [], [], [], [], [], [], []
    for l in range(DEPTH):
        yp = ffn_half_step(yp, ffn1_norm[l], ffn1_gate[l], ffn1_up[l], ffn1_down[l])
        q, k, v, u = project_in(rmsnorm(yp, mix_norm[l]), w_in[l])
        o_a = merge_by_denominator([dilated_branch_prompt(q, k, v, slopes, wdw, dil)
                                    for (wdw, dil) in DILATED_BRANCHES])
        u_ext = jnp.pad(u, ((0, 0), (CONV_WIDTH - 1, 0), (0, 0)))
        o_b = conv_branch(u_ext, conv_w[l], conv_b[l], conv_ln_g[l], conv_ln_b[l])
        yp = yp + jnp.concatenate([o_a.reshape(Bp, S, WIDTH_A).astype(yp.dtype), o_b], axis=-1) @ w_out[l]
        mk, mv = mem_kv(mem_prompt, mem_norm[l], w_ck[l], w_cv[l])
        yp = yp + cross_attend(rmsnorm(yp, xattn_norm[l]), mk, mv, w_cq[l], w_co[l])
        yp = ffn_half_step(yp, ffn2_norm[l], ffn2_gate[l], ffn2_up[l], ffn2_down[l])
        pk.append(k[:, S - keep:])
        pv.append(v[:, S - keep:])
        pc.append(u[:, S - (CONV_WIDTH - 1):])
        pmk.append(mk)
        pmv.append(mv)
        ys = ffn_half_step(ys, ffn1_norm[l], ffn1_gate[l], ffn1_up[l], ffn1_down[l])
        q, k, v, u = project_in(rmsnorm(ys, mix_norm[l]), w_in[l])
        k_all = jnp.concatenate([cache_win_k[l].astype(k.dtype), k], axis=1)
        v_all = jnp.concatenate([cache_win_v[l].astype(v.dtype), v], axis=1)
        o_a = merge_by_denominator([dilated_branch_sample(q, k_all, v_all, slopes, wdw, dil)
                                    for (wdw, dil) in DILATED_BRANCHES])
        u_ext = jnp.concatenate([cache_conv[l].astype(u.dtype), u], axis=1)
        o_b = conv_branch(u_ext, conv_w[l], conv_b[l], conv_ln_g[l], conv_ln_b[l])
        ys = ys + jnp.concatenate([o_a.reshape(Bs, T, WIDTH_A).astype(ys.dtype), o_b], axis=-1) @ w_out[l]
        ys = ys + cross_attend(rmsnorm(ys, xattn_norm[l]), cache_mem_k[l], cache_mem_v[l], w_cq[l], w_co[l])
        ys = ffn_half_step(ys, ffn2_norm[l], ffn2_gate[l], ffn2_up[l], ffn2_down[l])
        sk.append(k)
        sv.append(v)
        sc.append(u_ext[:, u_ext.shape[1] - (CONV_WIDTH - 1):])
    y_prompt = rmsnorm(yp, final_norm)
    y_sample = rmsnorm(ys, final_norm)
    return (y_prompt, y_sample, jnp.stack(pk), jnp.stack(pv), jnp.stack(pc), jnp.stack(pmk), jnp.stack(pmv),
            jnp.stack(sk), jnp.stack(sv), jnp.stack(sc))
```

```python
import functools

import jax
import jax.numpy as jnp
from jax import lax
from jax.experimental import pallas as pl
from jax.experimental.pallas import tpu as pltpu

D_MODEL = 1024
HEAD_DIM = 64
N_HEADS_A = 8
WIDTH_A = N_HEADS_A * HEAD_DIM
WIDTH_B = D_MODEL - WIDTH_A
DILATED_BRANCHES = ((128, 1), (512, 4), (2048, 16))
BAND = 128
CONV_WIDTH = 31
D_FF = 2816
N_MEM = 256
N_HEADS_X = 4
HEAD_DIM_X = D_MODEL // N_HEADS_X
EPS = 1e-6

LANES = 128
FF_CHUNK = 256
HALO = 32
NEG_BIG = -1e30
VMEM_LIMIT = 56 * 1024 * 1024

F32 = jnp.float32
BF16 = jnp.bfloat16


def _const_spec(shape):
    nd = len(shape)
    return pl.BlockSpec(shape, lambda *_: (0,) * nd, pipeline_mode=pl.Buffered(1))


def _params(*sem):
    return pltpu.CompilerParams(dimension_semantics=sem, vmem_limit_bytes=VMEM_LIMIT)


def _rms(x, g):
    return x * lax.rsqrt(jnp.mean(x * x, axis=-1, keepdims=True) + EPS) * g


def _mm(a, b):
    return jnp.dot(a, b, preferred_element_type=F32)


def _mm_nt(a, b):
    return lax.dot_general(a, b, (((1,), (1,)), ((), ())), preferred_element_type=F32)


def _swiglu_half_step(x, g_ref, wg_ref, wu_ref, wd_ref, act_ref):
    h = _rms(x, g_ref[...]).astype(BF16)
    for c in range(0, D_FF, FF_CHUNK):
        gate = _mm(h, wg_ref[:, c:c + FF_CHUNK])
        up = _mm(h, wu_ref[:, c:c + FF_CHUNK])
        act_ref[:, c:c + FF_CHUNK] = (gate * jax.nn.sigmoid(gate) * up).astype(BF16)
    return x + 0.5 * _mm(act_ref[...], wd_ref[...])


def _ffn_proj_kernel(x_ref, g_ref, wg_ref, wu_ref, wd_ref, gmix_ref, win_ref,
                     x1_ref, q_ref, k_ref, v_ref, u_ref, act_ref):
    x1 = _swiglu_half_step(x_ref[...], g_ref, wg_ref, wu_ref, wd_ref, act_ref)
    x1_ref[...] = x1
    h = _rms(x1, gmix_ref[...]).astype(BF16)
    w = WIDTH_A
    q_ref[...] = _mm(h, win_ref[:, 0:w]) * (HEAD_DIM ** -0.5)
    k_ref[...] = _mm(h, win_ref[:, w:2 * w])
    v_ref[...] = _mm(h, win_ref[:, 2 * w:3 * w])
    a = _mm(h, win_ref[:, 3 * w:3 * w + WIDTH_B])
    g = _mm(h, win_ref[:, 3 * w + WIDTH_B:3 * w + 2 * WIDTH_B])
    u_ref[...] = a * jax.nn.sigmoid(g)


def _ffn_proj(x, g, wg, wu, wd, gmix, win, tm):
    n = x.shape[0]
    row = lambda w: pl.BlockSpec((tm, w), lambda i: (i, 0))
    return pl.pallas_call(
        _ffn_proj_kernel,
        grid=(n // tm,),
        in_specs=[row(D_MODEL), _const_spec((1, D_MODEL)), _const_spec(wg.shape), _const_spec(wu.shape),
                  _const_spec(wd.shape), _const_spec((1, D_MODEL)), _const_spec(win.shape)],
        out_specs=[row(D_MODEL), row(WIDTH_A), row(WIDTH_A), row(WIDTH_A), row(WIDTH_B)],
        out_shape=[jax.ShapeDtypeStruct((n, D_MODEL), F32)] + [jax.ShapeDtypeStruct((n, WIDTH_A), F32)] * 4,
        scratch_shapes=[pltpu.VMEM((tm, D_FF), BF16)],
        compiler_params=_params("parallel"),
        name="ffn_proj",
    )(x, g, wg, wu, wd, gmix, win)


def _ffn_final_kernel(x_ref, g_ref, wg_ref, wu_ref, wd_ref, gfin_ref, y_ref, act_ref):
    x1 = _swiglu_half_step(x_ref[...], g_ref, wg_ref, wu_ref, wd_ref, act_ref)
    y_ref[...] = _rms(x1, gfin_ref[...])


def _ffn_final(x, g, wg, wu, wd, gfin, tm):
    n = x.shape[0]
    row = pl.BlockSpec((tm, D_MODEL), lambda i: (i, 0))
    return pl.pallas_call(
        _ffn_final_kernel,
        grid=(n // tm,),
        in_specs=[row, _const_spec((1, D_MODEL)), _const_spec(wg.shape), _const_spec(wu.shape),
                  _const_spec(wd.shape), _const_spec((1, D_MODEL))],
        out_specs=row,
        out_shape=jax.ShapeDtypeStruct((n, D_MODEL), F32),
        scratch_shapes=[pltpu.VMEM((tm, D_FF), BF16)],
        compiler_params=_params("parallel"),
        name="ffn_final",
    )(x, g, wg, wu, wd, gfin)


def _attn_prompt_kernel(slopes_ref, q_ref, k_ref, v_ref, o_ref,
                        qp_ref, kp_ref, vp_ref, bias_ref, m_ref, l_ref, n_ref, *, seq):
    pair = pl.program_id(1)
    lane = lax.broadcasted_iota(jnp.int32, (BAND, LANES), 1)
    lo = lane < HEAD_DIM

    for b, (_, dil) in enumerate(DILATED_BRANCHES):
        per = seq // dil
        for r in range(dil):
            rows = pl.ds(r, per, stride=dil) if dil > 1 else pl.ds(0, seq)
            qp_ref[b, r * per:(r + 1) * per, :] = q_ref[rows, :].astype(BF16)
            kp_ref[b, r * per:(r + 1) * per, :] = k_ref[rows, :].astype(BF16)
            vp_ref[b, r * per:(r + 1) * per, :] = v_ref[rows, :].astype(BF16)

    qi = lax.broadcasted_iota(jnp.int32, (BAND, 2 * BAND), 0)
    kj = lax.broadcasted_iota(jnp.int32, (BAND, 2 * BAND), 1)
    dist = qi + BAND - kj
    neg_dist = jnp.where((dist >= 0) & (dist <= BAND), -dist.astype(F32), NEG_BIG)
    for b, (_, dil) in enumerate(DILATED_BRANCHES):
        for half in range(2):
            bias_ref[2 * b + half] = (neg_dist * slopes_ref[2 * pair + half]) * float(dil)

    def block(b, n, has_prev):
        row0 = pl.multiple_of(n * BAND, BAND)
        qb = qp_ref[b, pl.ds(row0, BAND), :]
        if has_prev:
            keys = pl.ds(pl.multiple_of(n * BAND - BAND, BAND), 2 * BAND)
        else:
            keys = pl.ds(row0, BAND)
        kb = kp_ref[b, keys, :]
        vb = vp_ref[b, keys, :]
        stats = []
        for half in range(2):
            qm = jnp.where(lo if half == 0 else jnp.logical_not(lo), qb, jnp.zeros_like(qb))
            bias = bias_ref[2 * b + half] if has_prev else bias_ref[2 * b + half, :, BAND:2 * BAND]
            s = _mm_nt(qm, kb) + bias
            m = jnp.max(s, axis=-1, keepdims=True)
            p = jnp.exp(s - m)
            l = jnp.sum(p, axis=-1, keepdims=True)
            stats.append((m, l, _mm(p.astype(BF16), vb)))
        (m0, l0, n0), (m1, l1, n1) = stats
        m_ref[b, pl.ds(row0, BAND), :] = jnp.where(lo, m0, m1)
        l_ref[b, pl.ds(row0, BAND), :] = jnp.where(lo, l0, l1)
        n_ref[b, pl.ds(row0, BAND), :] = jnp.where(lo, n0, n1)

    for b, (_, dil) in enumerate(DILATED_BRANCHES):
        nblk = seq // dil // BAND
        if nblk == 1:
            def body(i, c, b=b):
                for j in range(4):
                    block(b, 4 * i + j, False)
                return c
            lax.fori_loop(0, dil // 4, body, 0)
        else:
            group = 3 if (nblk - 1) % 3 == 0 else 1
            def body(r, c, b=b, nblk=nblk, group=group):
                block(b, r * nblk, False)
                def inner(i, c2):
                    for j in range(group):
                        block(b, r * nblk + 1 + group * i + j, True)
                    return c2
                lax.fori_loop(0, (nblk - 1) // group, inner, 0)
                return c
            lax.fori_loop(0, dil, body, 0)

    per16 = seq // 16
    for r in range(16):
        sl = (pl.ds(r, per16, stride=16), pl.ds((r % 4) * (seq // 4) + r // 4, per16, stride=4),
              pl.ds(r * per16, per16))
        ms = [m_ref[b, sl[b], :] for b in range(3)]
        mx = jnp.maximum(jnp.maximum(ms[0], ms[1]), ms[2])
        es = [jnp.exp(m - mx) for m in ms]
        den = es[0] * l_ref[0, sl[0], :] + es[1] * l_ref[1, sl[1], :] + es[2] * l_ref[2, sl[2], :]
        num = es[0] * n_ref[0, sl[0], :] + es[1] * n_ref[1, sl[1], :] + es[2] * n_ref[2, sl[2], :]
        o_ref[sl[0], :] = num / den


def _attn_prompt(q, k, v, slopes, batch, seq):
    assert seq // 16 == BAND and seq % (4 * BAND) == 0
    blk = pl.BlockSpec((seq, LANES), lambda b, p: (b, p))
    return pl.pallas_call(
        functools.partial(_attn_prompt_kernel, seq=seq),
        grid=(batch, WIDTH_A // LANES),
        in_specs=[pl.BlockSpec(memory_space=pltpu.SMEM), blk, blk, blk],
        out_specs=blk,
        out_shape=jax.ShapeDtypeStruct((batch * seq, WIDTH_A), F32),
        scratch_shapes=[pltpu.VMEM((3, seq, LANES), BF16)] * 3
        + [pltpu.VMEM((6, BAND, 2 * BAND), F32)] + [pltpu.VMEM((3, seq, LANES), F32)] * 3,
        compiler_params=_params("parallel", "parallel"),
        name="attn_prompt",
    )(slopes, q, k, v)


def _attn_sample_kernel(q_ref, kn_ref, vn_ref, ck_ref, cv_ref, o_ref, kall_ref, vall_ref, *, t_new, w_buf, pad):
    rows = N_HEADS_A * t_new
    kall_ref[0:w_buf, :] = ck_ref[...].astype(BF16)
    vall_ref[0:w_buf, :] = cv_ref[...].astype(BF16)
    zeros = jnp.zeros((pad - t_new, WIDTH_A), F32)
    kall_ref[w_buf:w_buf + pad, :] = jnp.concatenate([kn_ref[...], zeros], axis=0).astype(BF16)
    vall_ref[w_buf:w_buf + pad, :] = jnp.concatenate([vn_ref[...], zeros], axis=0).astype(BF16)

    qrep = jnp.concatenate([q_ref[...]] * N_HEADS_A, axis=0)
    rr = lax.broadcasted_iota(jnp.int32, (rows, WIDTH_A), 0)
    ll = lax.broadcasted_iota(jnp.int32, (rows, WIDTH_A), 1)
    qrows = jnp.where(rr // t_new == ll // HEAD_DIM, qrep, 0.0).astype(BF16)
    s = _mm_nt(qrows, kall_ref[...])

    r2 = lax.broadcasted_iota(jnp.int32, s.shape, 0)
    u2 = lax.broadcasted_iota(jnp.int32, s.shape, 1)
    d = w_buf + r2 % t_new - u2
    mult = jnp.zeros(s.shape, F32)
    for window, dil in DILATED_BRANCHES:
        mult = mult + jnp.where((d >= 0) & (d <= window) & (d % dil == 0), 1.0, 0.0)
    head = lax.broadcasted_iota(jnp.int32, (rows, 1), 0) // t_new
    slope = jnp.zeros((rows, 1), F32)
    for h in range(N_HEADS_A):
        slope = jnp.where(head == h, 2.0 ** -(h + 1), slope)
    s = jnp.where(mult > 0, s - d.astype(F32) * slope, NEG_BIG)
    m = jnp.max(s, axis=-1, keepdims=True)
    p = mult * jnp.exp(s - m)
    den = jnp.sum(p, axis=-1, keepdims=True)
    o = _mm(p.astype(BF16), vall_ref[...]) * (1.0 / den)

    lane_head = lax.broadcasted_iota(jnp.int32, (t_new, WIDTH_A), 1) // HEAD_DIM
    out = jnp.zeros((t_new, WIDTH_A), F32)
    for h in range(N_HEADS_A):
        out = jnp.where(lane_head == h, o[h * t_new:(h + 1) * t_new, :], out)
    o_ref[...] = out


def _attn_sample(q, k, v, cache_k, cache_v, t_new):
    batch, w_buf, _ = cache_k.shape
    assert t_new % 8 == 0 and w_buf >= DILATED_BRANCHES[-1][0]
    pad = LANES
    new = pl.BlockSpec((t_new, WIDTH_A), lambda b: (b, 0))
    cache = pl.BlockSpec((None, w_buf, WIDTH_A), lambda b: (b, 0, 0))
    return pl.pallas_call(
        functools.partial(_attn_sample_kernel, t_new=t_new, w_buf=w_buf, pad=pad),
        grid=(batch,),
        in_specs=[new, new, new, cache, cache],
        out_specs=new,
        out_shape=jax.ShapeDtypeStruct((batch * t_new, WIDTH_A), F32),
        scratch_shapes=[pltpu.VMEM((w_buf + pad, WIDTH_A), BF16)] * 2,
        compiler_params=_params("parallel"),
        name="attn_sample",
    )(q, k, v, cache_k, cache_v)


def _mem_kv_kernel(mem_ref, g_ref, wk_ref, wv_ref, mk_ref, mv_ref):
    h = _rms(mem_ref[...], g_ref[...]).astype(BF16)
    mk_ref[...] = _mm(h, wk_ref[...])
    mv_ref[...] = _mm(h, wv_ref[...])


def _mem_kv(mem, g, wk, wv, tm):
    n = mem.shape[0]
    row = pl.BlockSpec((tm, D_MODEL), lambda i: (i, 0))
    return pl.pallas_call(
        _mem_kv_kernel,
        grid=(n // tm,),
        in_specs=[row, _const_spec((1, D_MODEL)), _const_spec(wk.shape), _const_spec(wv.shape)],
        out_specs=[row, row],
        out_shape=[jax.ShapeDtypeStruct((n, D_MODEL), F32)] * 2,
        compiler_params=_params("parallel"),
        name="mem_kv",
    )(mem, g, wk, wv)


def _conv_module(ext_ref, first_row, rows, cw_ref, cb_ref, lg_ref, lb_ref):
    y = cb_ref[...] + cw_ref[0:1, :] * ext_ref[pl.ds(first_row, rows), :]
    for j in range(1, CONV_WIDTH):
        y = y + cw_ref[j:j + 1, :] * ext_ref[pl.ds(first_row + j, rows), :]
    yc = y - jnp.mean(y, axis=-1, keepdims=True)
    yn = yc * lax.rsqrt(jnp.mean(yc * yc, axis=-1, keepdims=True) + EPS) * lg_ref[...] + lb_ref[...]
    return yn * jax.nn.sigmoid(yn)


def _softmax_rows(s):
    m = jnp.max(s, axis=-1, keepdims=True)
    p = jnp.exp(s - m)
    return p.astype(BF16), 1.0 / jnp.sum(p, axis=-1, keepdims=True)


def _mix_prompt_kernel(x_ref, u_ref, halo_ref, oa_ref, mk_ref, mv_ref, cw_ref, cb_ref, lg_ref, lb_ref,
                       wout_ref, gx_ref, wq_ref, wo_ref, y_ref, ext_ref):
    ts = x_ref.shape[0]
    halo = halo_ref[...]
    ext_ref[0:HALO, :] = jnp.where(pl.program_id(1) == 0, jnp.zeros_like(halo), halo)
    ext_ref[HALO:HALO + ts, :] = u_ref[...]
    ob = _conv_module(ext_ref, HALO - (CONV_WIDTH - 1), ts, cw_ref, cb_ref, lg_ref, lb_ref)
    x2 = (x_ref[...] + _mm(oa_ref[...].astype(BF16), wout_ref[0:WIDTH_A, :])
          + _mm(ob.astype(BF16), wout_ref[WIDTH_A:WIDTH_A + WIDTH_B, :]))

    hx = _rms(x2, gx_ref[...]).astype(BF16)
    qx = (_mm(hx, wq_ref[...]) * (HEAD_DIM_X ** -0.5)).astype(BF16)
    outs = []
    for h in range(N_HEADS_X):
        sl = slice(h * HEAD_DIM_X, (h + 1) * HEAD_DIM_X)
        p, inv = _softmax_rows(_mm_nt(qx[:, sl], mk_ref[:, sl].astype(BF16)))
        outs.append((_mm(p, mv_ref[:, sl].astype(BF16)) * inv).astype(BF16))
    y_ref[...] = x2 + _mm(jnp.concatenate(outs, axis=-1), wo_ref[...])


def _mix_prompt(x, u, oa, mk, mv, cw, cb, lg, lb, wout, gx, wq, wo, batch, seq, ts):
    tiles = seq // ts
    row = lambda w: pl.BlockSpec((ts, w), lambda b, i: (b * tiles + i, 0))
    halo = pl.BlockSpec((HALO, WIDTH_B),
                        lambda b, i: (jnp.maximum((b * seq + i * ts) // HALO - 1, 0), 0))
    mem = pl.BlockSpec((N_MEM, D_MODEL), lambda b, i: (b, 0))
    vec = lambda w: _const_spec((1, w))
    return pl.pallas_call(
        _mix_prompt_kernel,
        grid=(batch, tiles),
        in_specs=[row(D_MODEL), row(WIDTH_B), halo, row(WIDTH_A), mem, mem, _const_spec(cw.shape),
                  vec(WIDTH_B), vec(WIDTH_B), vec(WIDTH_B), _const_spec(wout.shape), vec(D_MODEL),
                  _const_spec(wq.shape), _const_spec(wo.shape)],
        out_specs=row(D_MODEL),
        out_shape=jax.ShapeDtypeStruct((batch * seq, D_MODEL), F32),
        scratch_shapes=[pltpu.VMEM((HALO + ts, WIDTH_B), F32)],
        compiler_params=_params("parallel", "parallel"),
        name="mix_prompt",
    )(x, u, u, oa, mk, mv, cw, cb, lg, lb, wout, gx, wq, wo)


def _mix_sample_kernel(x_ref, uext_ref, oa_ref, mk_ref, mv_ref, cw_ref, cb_ref, lg_ref, lb_ref,
                       wout_ref, gx_ref, wq_ref, wo_ref, y_ref, *, group, t_new, t_ext):
    rows = group * t_ext
    conv = _conv_module(uext_ref, 0, rows, cw_ref, cb_ref, lg_ref, lb_ref)
    ob = conv.reshape(group, t_ext, WIDTH_B)[:, 0:t_new, :].reshape(group * t_new, WIDTH_B)
    x2 = (x_ref[...] + _mm(oa_ref[...].astype(BF16), wout_ref[0:WIDTH_A, :])
          + _mm(ob.astype(BF16), wout_ref[WIDTH_A:WIDTH_A + WIDTH_B, :]))

    hx = _rms(x2, gx_ref[...]).astype(BF16)
    qx = _mm(hx, wq_ref[...]) * (HEAD_DIM_X ** -0.5)
    qrows_n = N_HEADS_X * t_new
    rr = lax.broadcasted_iota(jnp.int32, (qrows_n, D_MODEL), 0)
    ll = lax.broadcasted_iota(jnp.int32, (qrows_n, D_MODEL), 1)
    own = rr // t_new == ll // HEAD_DIM_X
    lane_head = lax.broadcasted_iota(jnp.int32, (t_new, D_MODEL), 1) // HEAD_DIM_X
    outs = []
    for b in range(group):
        qb = qx[b * t_new:(b + 1) * t_new, :]
        qrows = jnp.where(own, jnp.concatenate([qb] * N_HEADS_X, axis=0), 0.0).astype(BF16)
        p, inv = _softmax_rows(_mm_nt(qrows, mk_ref[b].astype(BF16)))
        o = _mm(p, mv_ref[b].astype(BF16)) * inv
        out = jnp.zeros((t_new, D_MODEL), F32)
        for h in range(N_HEADS_X):
            out = jnp.where(lane_head == h, o[h * t_new:(h + 1) * t_new, :], out)
        outs.append(out)
    y_ref[...] = x2 + _mm(jnp.concatenate(outs, axis=0).astype(BF16), wo_ref[...])


def _mix_sample(x, uext, oa, mk, mv, cw, cb, lg, lb, wout, gx, wq, wo, batch, t_new, t_ext, group):
    rows = group * t_new
    row = lambda w: pl.BlockSpec((rows, w), lambda i: (i, 0))
    ext = pl.BlockSpec((group * t_ext + HALO, WIDTH_B), lambda i: (i, 0))
    mem = pl.BlockSpec((group, N_MEM, D_MODEL), lambda i: (i, 0, 0))
    vec = lambda w: _const_spec((1, w))
    return pl.pallas_call(
        functools.partial(_mix_sample_kernel, group=group, t_new=t_new, t_ext=t_ext),
        grid=(batch // group,),
        in_specs=[row(D_MODEL), ext, row(WIDTH_A), mem, mem, _const_spec(cw.shape),
                  vec(WIDTH_B), vec(WIDTH_B), vec(WIDTH_B), _const_spec(wout.shape), vec(D_MODEL),
                  _const_spec(wq.shape), _const_spec(wo.shape)],
        out_specs=row(D_MODEL),
        out_shape=jax.ShapeDtypeStruct((batch * t_new, D_MODEL), F32),
        compiler_params=_params("parallel"),
        name="mix_sample",
    )(x, uext, oa, mk, mv, cw, cb, lg, lb, wout, gx, wq, wo)


def kernel(x_prompt, x_sample, mem_prompt, cache_win_k, cache_win_v, cache_conv, cache_mem_k, cache_mem_v, ffn1_norm, ffn1_gate, ffn1_up, ffn1_down, mix_norm, w_in, conv_w, conv_b, conv_ln_g, conv_ln_b, w_out, xattn_norm, mem_norm, w_cq, w_ck, w_cv, w_co, ffn2_norm, ffn2_gate, ffn2_up, ffn2_down, final_norm):
    depth = ffn1_norm.shape[0]
    assert depth == 1
    bp, seq, _ = x_prompt.shape
    bs, t_new, _ = x_sample.shape
    w_buf = cache_win_k.shape[2]
    keep = CONV_WIDTH - 1
    l = 0
    vec = lambda a: a.reshape(1, -1)
    bf = lambda a: a.astype(BF16)
    slopes = jnp.asarray([2.0 ** -(h + 1) for h in range(N_HEADS_A)], F32)

    f1 = (vec(ffn1_norm[l]), bf(ffn1_gate[l]), bf(ffn1_up[l]), bf(ffn1_down[l]))
    f2 = (vec(ffn2_norm[l]), bf(ffn2_gate[l]), bf(ffn2_up[l]), bf(ffn2_down[l]))
    gmix, win = vec(mix_norm[l]), bf(w_in[l])
    conv = (conv_w[l], vec(conv_b[l]), vec(conv_ln_g[l]), vec(conv_ln_b[l]))
    proj = (bf(w_out[l]), vec(xattn_norm[l]), bf(w_cq[l]), bf(w_co[l]))
    gfin = vec(final_norm)

    xp = x_prompt.reshape(bp * seq, D_MODEL)
    x1, q, k, v, u = _ffn_proj(xp, *f1, gmix, win, tm=512)
    oa = _attn_prompt(q, k, v, slopes, bp, seq)
    mk, mv = _mem_kv(mem_prompt.reshape(bp * N_MEM, D_MODEL), vec(mem_norm[l]), bf(w_ck[l]), bf(w_cv[l]), tm=512)
    x3 = _mix_prompt(x1, u, oa, mk, mv, *conv, *proj, batch=bp, seq=seq, ts=512)
    yp = _ffn_final(x3, *f2, gfin, tm=512)

    xs = x_sample.reshape(bs * t_new, D_MODEL)
    s1, sq, sk, sv, su = _ffn_proj(xs, *f1, gmix, win, tm=bs * t_new)
    soa = _attn_sample(sq, sk, sv, cache_win_k[l].reshape(bs, w_buf, WIDTH_A),
                       cache_win_v[l].reshape(bs, w_buf, WIDTH_A), t_new)
    u_ext = jnp.concatenate([cache_conv[l], su.reshape(bs, t_new, WIDTH_B)], axis=1)
    t_ext = -(-(keep + t_new) // 8) * 8
    group = 8
    u_pad = jnp.pad(u_ext, ((0, 0), (0, t_ext - keep - t_new), (0, 0))).reshape(bs // group, group * t_ext, WIDTH_B)
    u_pad = jnp.pad(u_pad, ((0, 0), (0, HALO), (0, 0))).reshape(-1, WIDTH_B)
    s3 = _mix_sample(s1, u_pad, soa, cache_mem_k[l].reshape(bs, N_MEM, D_MODEL),
                     cache_mem_v[l].reshape(bs, N_MEM, D_MODEL), *conv, *proj,
                     batch=bs, t_new=t_new, t_ext=t_ext, group=group)
    ys = _ffn_final(s3, *f2, gfin, tm=bs * t_new)

    heads = lambda a, b, t: a.reshape(1, b, t, N_HEADS_A, HEAD_DIM)
    return (yp.reshape(bp, seq, D_MODEL),
            ys.reshape(bs, t_new, D_MODEL),
            heads(k, bp, seq), heads(v, bp, seq),
            u.reshape(1, bp, seq, WIDTH_B)[:, :, seq - keep:],
            mk.reshape(1, bp, N_MEM, N_HEADS_X, HEAD_DIM_X), mv.reshape(1, bp, N_MEM, N_HEADS_X, HEAD_DIM_X),
            heads(sk, bs, t_new), heads(sv, bs, t_new),
            u_ext[None, :, t_new:])
```

```python
import functools

import jax
import jax.numpy as jnp
from jax import lax
from jax.experimental import pallas as pl
from jax.experimental.pallas import tpu as pltpu

D_MODEL = 1024
HEAD_DIM = 64
N_HEADS_A = 8
WIDTH_A = N_HEADS_A * HEAD_DIM
WIDTH_B = D_MODEL - WIDTH_A
DILATED_BRANCHES = ((128, 1), (512, 4), (2048, 16))
BAND = 128
CONV_WIDTH = 31
D_FF = 2816
N_MEM = 256
N_HEADS_X = 4
HEAD_DIM_X = D_MODEL // N_HEADS_X
EPS = 1e-6

LANES = 128
SUBLANES = 8
FF_CHUNK = 256
HALO = 32
MIX_ROWS = 512
NEG_BIG = -1e30
VMEM_LIMIT = 56 * 1024 * 1024

F32 = jnp.float32
BF16 = jnp.bfloat16


def _const_spec(shape):
    nd = len(shape)
    return pl.BlockSpec(shape, lambda *_: (0,) * nd, pipeline_mode=pl.Buffered(1))


def _params(*sem):
    return pltpu.CompilerParams(dimension_semantics=sem, vmem_limit_bytes=VMEM_LIMIT)


def _rms(x, g):
    return x * lax.rsqrt(jnp.mean(x * x, axis=-1, keepdims=True) + EPS) * g


def _mm(a, b):
    return jnp.dot(a, b, preferred_element_type=F32)


def _mm_nt(a, b):
    return lax.dot_general(a, b, (((1,), (1,)), ((), ())), preferred_element_type=F32)


def _swiglu_half_step(x, g_ref, wg_ref, wu_ref, wd_ref, act_ref):
    h = _rms(x, g_ref[...]).astype(BF16)
    for c in range(0, D_FF, FF_CHUNK):
        gate = _mm(h, wg_ref[:, c:c + FF_CHUNK])
        up = _mm(h, wu_ref[:, c:c + FF_CHUNK])
        act_ref[:, c:c + FF_CHUNK] = (gate * jax.nn.sigmoid(gate) * up).astype(BF16)
    return x + 0.5 * _mm(act_ref[...], wd_ref[...])


def _ffn_proj_kernel(x_ref, g_ref, wg_ref, wu_ref, wd_ref, gmix_ref, win_ref,
                     x1_ref, q_ref, k_ref, v_ref, u_ref, act_ref):
    x1 = _swiglu_half_step(x_ref[...], g_ref, wg_ref, wu_ref, wd_ref, act_ref)
    x1_ref[...] = x1
    h = _rms(x1, gmix_ref[...]).astype(BF16)
    w = WIDTH_A
    q_ref[...] = _mm(h, win_ref[:, 0:w]) * (HEAD_DIM ** -0.5)
    k_ref[...] = _mm(h, win_ref[:, w:2 * w])
    v_ref[...] = _mm(h, win_ref[:, 2 * w:3 * w])
    a = _mm(h, win_ref[:, 3 * w:3 * w + WIDTH_B])
    g = _mm(h, win_ref[:, 3 * w + WIDTH_B:3 * w + 2 * WIDTH_B])
    u_ref[...] = a * jax.nn.sigmoid(g)


def _ffn_proj(x, g, wg, wu, wd, gmix, win, tm):
    n = x.shape[0]
    row = lambda w: pl.BlockSpec((tm, w), lambda i: (i, 0))
    return pl.pallas_call(
        _ffn_proj_kernel,
        grid=(n // tm,),
        in_specs=[row(D_MODEL), _const_spec((1, D_MODEL)), _const_spec(wg.shape), _const_spec(wu.shape),
                  _const_spec(wd.shape), _const_spec((1, D_MODEL)), _const_spec(win.shape)],
        out_specs=[row(D_MODEL), row(WIDTH_A), row(WIDTH_A), row(WIDTH_A), row(WIDTH_B)],
        out_shape=[jax.ShapeDtypeStruct((n, D_MODEL), F32)] + [jax.ShapeDtypeStruct((n, WIDTH_A), F32)] * 4,
        scratch_shapes=[pltpu.VMEM((tm, D_FF), BF16)],
        compiler_params=_params("parallel"),
        name="ffn_proj",
    )(x, g, wg, wu, wd, gmix, win)


def _ffn_final_kernel(x_ref, g_ref, wg_ref, wu_ref, wd_ref, gfin_ref, y_ref, act_ref):
    x1 = _swiglu_half_step(x_ref[...], g_ref, wg_ref, wu_ref, wd_ref, act_ref)
    y_ref[...] = _rms(x1, gfin_ref[...])


def _ffn_final(x, g, wg, wu, wd, gfin, tm):
    n = x.shape[0]
    row = pl.BlockSpec((tm, D_MODEL), lambda i: (i, 0))
    return pl.pallas_call(
        _ffn_final_kernel,
        grid=(n // tm,),
        in_specs=[row, _const_spec((1, D_MODEL)), _const_spec(wg.shape), _const_spec(wu.shape),
                  _const_spec(wd.shape), _const_spec((1, D_MODEL))],
        out_specs=row,
        out_shape=jax.ShapeDtypeStruct((n, D_MODEL), F32),
        scratch_shapes=[pltpu.VMEM((tm, D_FF), BF16)],
        compiler_params=_params("parallel"),
        name="ffn_final",
    )(x, g, wg, wu, wd, gfin)


def _attn_prompt_kernel(slopes_ref, q_ref, k_ref, v_ref, o_ref,
                        x4_ref, qs_ref, kp_ref, vp_ref, bias_ref, s_ref, p_ref, m_ref, l_ref, n_ref,
                        fm_ref, fl_ref, fn_ref, *, seq):
    pair = pl.program_id(1)
    nb = seq // BAND
    lo = lax.broadcasted_iota(jnp.int32, (BAND, LANES), 1) < HEAD_DIM

    @pl.when(pair == 0)
    def _():
        kp_ref[:, 0:BAND, :] = jnp.zeros((3, BAND, LANES), BF16)
        vp_ref[:, 0:BAND, 0:LANES] = jnp.zeros((3, BAND, LANES), BF16)
        vp_ref[:, :, LANES:2 * LANES] = jnp.ones((3, BAND + seq, LANES), BF16)
        p_ref[...] = jnp.zeros(p_ref.shape, BF16)

    def put(kind, b, first, x):
        nblocks = x.shape[0] // BAND
        if kind == 0:
            is_lo = lax.broadcasted_iota(jnp.int32, x.shape, 1) < HEAD_DIM
            x_lo = jnp.where(is_lo, x, 0.0).astype(BF16)
            x_hi = jnp.where(is_lo, 0.0, x).astype(BF16)
            for j in range(nblocks):
                qs_ref[b, first + j, 0] = x_lo[j * BAND:(j + 1) * BAND]
                qs_ref[b, first + j, 1] = x_hi[j * BAND:(j + 1) * BAND]
        elif kind == 1:
            kp_ref[b, BAND + first * BAND:BAND + first * BAND + x.shape[0], :] = x.astype(BF16)
        else:
            vp_ref[b, BAND + first * BAND:BAND + first * BAND + x.shape[0], 0:LANES] = x.astype(BF16)

    quarter = seq // 4
    for kind, ref in enumerate((q_ref, k_ref, v_ref)):
        for r in range(4):
            put(kind, 0, 4 * r, ref[r * quarter:(r + 1) * quarter, :])
            x = ref[pl.ds(r, quarter, stride=4), :]
            x4_ref[r * quarter:(r + 1) * quarter, :] = x
            put(kind, 1, 4 * r, x)
        for r in range(16):
            put(kind, 2, r, x4_ref[pl.ds((r % 4) * quarter + r // 4, seq // 16, stride=4), :])

    qi = lax.broadcasted_iota(jnp.int32, (BAND, 2 * BAND), 0)
    kj = lax.broadcasted_iota(jnp.int32, (BAND, 2 * BAND), 1)
    dist = qi + BAND - kj
    neg_dist = jnp.where((dist >= 0) & (dist <= BAND), -dist.astype(F32), NEG_BIG)
    neg_dist_cur = jnp.where(kj >= BAND, neg_dist, NEG_BIG)
    for b, (_, dil) in enumerate(DILATED_BRANCHES):
        for noprev in range(2):
            if 2 * b + noprev < bias_ref.shape[0]:
                table = neg_dist_cur if noprev else neg_dist
                for half in range(2):
                    bias_ref[2 * b + noprev, half * BAND:(half + 1) * BAND, :] = (
                        (table * slopes_ref[2 * pair + half]) * float(dil))

    def blocks_per_class(b):
        return seq // DILATED_BRANCHES[b][1] // BAND

    def scores(n):
        out = []
        for b in range(3):
            qb = qs_ref[b, n].reshape(2 * BAND, LANES)
            if blocks_per_class(b) > 1:
                kb = kp_ref[b, pl.ds(pl.multiple_of(n * BAND, BAND), 2 * BAND), :]
                bias = bias_ref[2 * b + (n % blocks_per_class(b) == 0).astype(jnp.int32)]
            else:
                kb = kp_ref[b, pl.ds(pl.multiple_of(n * BAND + BAND, BAND), BAND), :]
                bias = bias_ref[2 * b, :, BAND:2 * BAND]
            out.append(_mm_nt(qb, kb) + bias)
        return out

    def keys_of(b):
        return 2 * BAND if blocks_per_class(b) > 1 else BAND

    def softmax(n):
        rows = pl.ds(pl.multiple_of(n * BAND, BAND), BAND)
        for b in range(3):
            s = s_ref[b, :, 0:keys_of(b)]
            m = jnp.max(s, axis=-1, keepdims=True)
            p_ref[b, :, 0:keys_of(b)] = jnp.exp(s - m).astype(BF16)
            m_ref[b, rows, :] = jnp.where(lo, m[0:BAND], m[BAND:2 * BAND])

    def weighted_values(n):
        out = []
        for b in range(3):
            first = n * BAND + (2 * BAND - keys_of(b))
            vb = vp_ref[b, pl.ds(pl.multiple_of(first, BAND), keys_of(b)), :]
            out.append(_mm(p_ref[b, :, 0:keys_of(b)], vb))
        return out

    def step(n, carry):
        s_next = scores(jnp.minimum(n + 1, nb - 1))
        done = jnp.maximum(n - 1, 0)
        pv = weighted_values(done)
        softmax(jnp.minimum(n, nb - 1))
        rows = pl.ds(pl.multiple_of(done * BAND, BAND), BAND)
        for b in range(3):
            s_ref[b, :, 0:keys_of(b)] = s_next[b]
            n_ref[b, rows, :] = jnp.where(lo, pv[b][0:BAND, 0:LANES], pv[b][BAND:2 * BAND, 0:LANES])
            l_ref[b, rows, :] = jnp.where(lo, pv[b][0:BAND, LANES:2 * LANES], pv[b][BAND:2 * BAND, LANES:2 * LANES])
        return carry

    for b, s0 in enumerate(scores(jnp.int32(0))):
        s_ref[b, :, 0:keys_of(b)] = s0
    lax.fori_loop(0, nb + 1, step, 0)

    per16 = seq // 16
    for r in range(16):
        sl4 = pl.ds((r % 4) * quarter + r // 4, per16, stride=4)
        blk = pl.ds(r * per16, per16)
        m1, m2 = m_ref[1, sl4, :], m_ref[2, blk, :]
        mx = jnp.maximum(m1, m2)
        e1, e2 = jnp.exp(m1 - mx), jnp.exp(m2 - mx)
        fm_ref[sl4, :] = mx
        fl_ref[sl4, :] = e1 * l_ref[1, sl4, :] + e2 * l_ref[2, blk, :]
        fn_ref[sl4, :] = e1 * n_ref[1, sl4, :] + e2 * n_ref[2, blk, :]
    for r in range(4):
        for c in range(quarter // BAND):
            sl = pl.ds(r + 4 * BAND * c, BAND, stride=4)
            blk = pl.ds(r * quarter + c * BAND, BAND)
            m0, m1 = m_ref[0, sl, :], fm_ref[blk, :]
            mx = jnp.maximum(m0, m1)
            e0, e1 = jnp.exp(m0 - mx), jnp.exp(m1 - mx)
            den = e0 * l_ref[0, sl, :] + e1 * fl_ref[blk, :]
            num = e0 * n_ref[0, sl, :] + e1 * fn_ref[blk, :]
            o_ref[sl, :] = num / den


def _attn_prompt(q, k, v, slopes, batch, seq):
    assert [seq // d // BAND for _, d in DILATED_BRANCHES] == [16, 4, 1]
    nb = seq // BAND
    blk = pl.BlockSpec((seq, LANES), lambda b, p: (b, p))
    return pl.pallas_call(
        functools.partial(_attn_prompt_kernel, seq=seq),
        grid=(batch, WIDTH_A // LANES),
        in_specs=[pl.BlockSpec(memory_space=pltpu.SMEM), blk, blk, blk],
        out_specs=blk,
        out_shape=jax.ShapeDtypeStruct((batch * seq, WIDTH_A), F32),
        scratch_shapes=[pltpu.VMEM((seq, LANES), F32),
                        pltpu.VMEM((3, nb, 2, BAND, LANES), BF16),
                        pltpu.VMEM((3, BAND + seq, LANES), BF16),
                        pltpu.VMEM((3, BAND + seq, 2 * LANES), BF16),
                        pltpu.VMEM((5, 2 * BAND, 2 * BAND), F32),
                        pltpu.VMEM((3, 2 * BAND, 2 * BAND), F32),
                        pltpu.VMEM((3, 2 * BAND, 2 * BAND), BF16)]
        + [pltpu.VMEM((3, seq, LANES), F32)] * 3 + [pltpu.VMEM((seq, LANES), F32)] * 3,
        compiler_params=_params("parallel", "arbitrary"),
        name="attn_prompt",
    )(slopes, q, k, v)


def _attn_sample_kernel(q_ref, kn_ref, vn_ref, ck_ref, cv_ref, o_ref, kall_ref, vall_ref, *, t_new, w_buf, pad):
    rows = N_HEADS_A * t_new
    kall_ref[0:w_buf, :] = ck_ref[...].astype(BF16)
    vall_ref[0:w_buf, :] = cv_ref[...].astype(BF16)
    zeros = jnp.zeros((pad - t_new, WIDTH_A), F32)
    kall_ref[w_buf:w_buf + pad, :] = jnp.concatenate([kn_ref[...], zeros], axis=0).astype(BF16)
    vall_ref[w_buf:w_buf + pad, :] = jnp.concatenate([vn_ref[...], zeros], axis=0).astype(BF16)

    qrep = jnp.concatenate([q_ref[...]] * N_HEADS_A, axis=0)
    rr = lax.broadcasted_iota(jnp.int32, (rows, WIDTH_A), 0)
    ll = lax.broadcasted_iota(jnp.int32, (rows, WIDTH_A), 1)
    qrows = jnp.where(rr // t_new == ll // HEAD_DIM, qrep, 0.0).astype(BF16)
    s = _mm_nt(qrows, kall_ref[...])

    r2 = lax.broadcasted_iota(jnp.int32, s.shape, 0)
    u2 = lax.broadcasted_iota(jnp.int32, s.shape, 1)
    d = w_buf + r2 % t_new - u2
    mult = jnp.zeros(s.shape, F32)
    for window, dil in DILATED_BRANCHES:
        mult = mult + jnp.where((d >= 0) & (d <= window) & (d % dil == 0), 1.0, 0.0)
    head = lax.broadcasted_iota(jnp.int32, (rows, 1), 0) // t_new
    slope = jnp.zeros((rows, 1), F32)
    for h in range(N_HEADS_A):
        slope = jnp.where(head == h, 2.0 ** -(h + 1), slope)
    s = jnp.where(mult > 0, s - d.astype(F32) * slope, NEG_BIG)
    m = jnp.max(s, axis=-1, keepdims=True)
    p = mult * jnp.exp(s - m)
    den = jnp.sum(p, axis=-1, keepdims=True)
    o = _mm(p.astype(BF16), vall_ref[...]) * (1.0 / den)

    lane_head = lax.broadcasted_iota(jnp.int32, (t_new, WIDTH_A), 1) // HEAD_DIM
    out = jnp.zeros((t_new, WIDTH_A), F32)
    for h in range(N_HEADS_A):
        out = jnp.where(lane_head == h, o[h * t_new:(h + 1) * t_new, :], out)
    o_ref[...] = out


def _attn_sample(q, k, v, cache_k, cache_v, t_new):
    batch, w_buf, _ = cache_k.shape
    assert t_new % 8 == 0 and w_buf >= DILATED_BRANCHES[-1][0]
    pad = LANES
    new = pl.BlockSpec((t_new, WIDTH_A), lambda b: (b, 0))
    cache = pl.BlockSpec((None, w_buf, WIDTH_A), lambda b: (b, 0, 0))
    return pl.pallas_call(
        functools.partial(_attn_sample_kernel, t_new=t_new, w_buf=w_buf, pad=pad),
        grid=(batch,),
        in_specs=[new, new, new, cache, cache],
        out_specs=new,
        out_shape=jax.ShapeDtypeStruct((batch * t_new, WIDTH_A), F32),
        scratch_shapes=[pltpu.VMEM((w_buf + pad, WIDTH_A), BF16)] * 2,
        compiler_params=_params("parallel"),
        name="attn_sample",
    )(q, k, v, cache_k, cache_v)


def _mem_kv_kernel(mem_ref, g_ref, wk_ref, wv_ref, mk_ref, mv_ref):
    h = _rms(mem_ref[...], g_ref[...]).astype(BF16)
    mk_ref[...] = _mm(h, wk_ref[...])
    mv_ref[...] = _mm(h, wv_ref[...])


def _mem_kv(mem, g, wk, wv, tm):
    n = mem.shape[0]
    row = pl.BlockSpec((tm, D_MODEL), lambda i: (i, 0))
    return pl.pallas_call(
        _mem_kv_kernel,
        grid=(n // tm,),
        in_specs=[row, _const_spec((1, D_MODEL)), _const_spec(wk.shape), _const_spec(wv.shape)],
        out_specs=[row, row],
        out_shape=[jax.ShapeDtypeStruct((n, D_MODEL), F32)] * 2,
        compiler_params=_params("parallel"),
        name="mem_kv",
    )(mem, g, wk, wv)


def _shifted_rows(first_row, rows):
    offs = [first_row + j for j in range(CONV_WIDTH) if (first_row + j) % SUBLANES]
    return max(offs) // SUBLANES * SUBLANES + rows


def _realign_conv_input(ext_ref, sh_ref):
    for s in range(1, SUBLANES):
        sh_ref[s - 1] = ext_ref[pl.ds(s, sh_ref.shape[1]), :]


def _conv_module(ext_ref, sh_ref, first_row, row0, rows, cw_ref, cb_ref, lg_ref, lb_ref):
    y = cb_ref[...]
    for j in range(CONV_WIDTH):
        base, s = (first_row + j) // SUBLANES * SUBLANES, (first_row + j) % SUBLANES
        at = pl.ds(base + row0, rows)
        y = y + cw_ref[j:j + 1, :] * (ext_ref[at, :] if s == 0 else sh_ref[s - 1, at, :])
    yc = y - jnp.mean(y, axis=-1, keepdims=True)
    yn = yc * lax.rsqrt(jnp.mean(yc * yc, axis=-1, keepdims=True) + EPS) * lg_ref[...] + lb_ref[...]
    return yn * jax.nn.sigmoid(yn)


def _softmax_rows(s):
    m = jnp.max(s, axis=-1, keepdims=True)
    p = jnp.exp(s - m)
    return p.astype(BF16), 1.0 / jnp.sum(p, axis=-1, keepdims=True)


def _mix_prompt_kernel(x_ref, u_ref, halo_ref, oa_ref, mk_ref, mv_ref, cw_ref, cb_ref, lg_ref, lb_ref,
                       wout_ref, gx_ref, wq_ref, wo_ref, y_ref, ext_ref, sh_ref):
    ts = x_ref.shape[0]
    halo = halo_ref[...]
    ext_ref[0:HALO, :] = jnp.where(pl.program_id(1) == 0, jnp.zeros_like(halo), halo)
    ext_ref[HALO:HALO + ts, :] = u_ref[...]
    _realign_conv_input(ext_ref, sh_ref)
    mk = mk_ref[...].astype(BF16)
    mv = mv_ref[...].astype(BF16)
    for r0 in range(0, ts, MIX_ROWS):
        rows = slice(r0, r0 + MIX_ROWS)
        ob = _conv_module(ext_ref, sh_ref, HALO - (CONV_WIDTH - 1), r0, MIX_ROWS, cw_ref, cb_ref, lg_ref, lb_ref)
        x2 = (x_ref[rows, :] + _mm(oa_ref[rows, :].astype(BF16), wout_ref[0:WIDTH_A, :])
              + _mm(ob.astype(BF16), wout_ref[WIDTH_A:WIDTH_A + WIDTH_B, :]))
        hx = _rms(x2, gx_ref[...]).astype(BF16)
        qx = (_mm(hx, wq_ref[...]) * (HEAD_DIM_X ** -0.5)).astype(BF16)
        outs = []
        for h in range(N_HEADS_X):
            sl = slice(h * HEAD_DIM_X, (h + 1) * HEAD_DIM_X)
            p, inv = _softmax_rows(_mm_nt(qx[:, sl], mk[:, sl]))
            outs.append((_mm(p, mv[:, sl]) * inv).astype(BF16))
        y_ref[rows, :] = x2 + _mm(jnp.concatenate(outs, axis=-1), wo_ref[...])


def _mix_prompt(x, u, oa, mk, mv, cw, cb, lg, lb, wout, gx, wq, wo, batch, seq, ts):
    tiles = seq // ts
    row = lambda w: pl.BlockSpec((ts, w), lambda b, i: (b * tiles + i, 0))
    halo = pl.BlockSpec((HALO, WIDTH_B),
                        lambda b, i: (jnp.maximum((b * seq + i * ts) // HALO - 1, 0), 0))
    mem = pl.BlockSpec((N_MEM, D_MODEL), lambda b, i: (b, 0))
    vec = lambda w: _const_spec((1, w))
    return pl.pallas_call(
        _mix_prompt_kernel,
        grid=(batch, tiles),
        in_specs=[row(D_MODEL), row(WIDTH_B), halo, row(WIDTH_A), mem, mem, _const_spec(cw.shape),
                  vec(WIDTH_B), vec(WIDTH_B), vec(WIDTH_B), _const_spec(wout.shape), vec(D_MODEL),
                  _const_spec(wq.shape), _const_spec(wo.shape)],
        out_specs=row(D_MODEL),
        out_shape=jax.ShapeDtypeStruct((batch * seq, D_MODEL), F32),
        scratch_shapes=[pltpu.VMEM((HALO + ts, WIDTH_B), F32),
                        pltpu.VMEM((SUBLANES - 1, _shifted_rows(HALO - (CONV_WIDTH - 1), ts), WIDTH_B), F32)],
        compiler_params=_params("parallel", "parallel"),
        name="mix_prompt",
    )(x, u, u, oa, mk, mv, cw, cb, lg, lb, wout, gx, wq, wo)


def _mix_sample_kernel(x_ref, uext_ref, oa_ref, mk_ref, mv_ref, cw_ref, cb_ref, lg_ref, lb_ref,
                       wout_ref, gx_ref, wq_ref, wo_ref, y_ref, sh_ref, *, group, t_new, t_ext):
    rows = group * t_ext
    _realign_conv_input(uext_ref, sh_ref)
    conv = _conv_module(uext_ref, sh_ref, 0, 0, rows, cw_ref, cb_ref, lg_ref, lb_ref)
    ob = conv.reshape(group, t_ext, WIDTH_B)[:, 0:t_new, :].reshape(group * t_new, WIDTH_B)
    x2 = (x_ref[...] + _mm(oa_ref[...].astype(BF16), wout_ref[0:WIDTH_A, :])
          + _mm(ob.astype(BF16), wout_ref[WIDTH_A:WIDTH_A + WIDTH_B, :]))

    hx = _rms(x2, gx_ref[...]).astype(BF16)
    qx = _mm(hx, wq_ref[...]) * (HEAD_DIM_X ** -0.5)
    qrows_n = N_HEADS_X * t_new
    rr = lax.broadcasted_iota(jnp.int32, (qrows_n, D_MODEL), 0)
    ll = lax.broadcasted_iota(jnp.int32, (qrows_n, D_MODEL), 1)
    own = rr // t_new == ll // HEAD_DIM_X
    lane_head = lax.broadcasted_iota(jnp.int32, (t_new, D_MODEL), 1) // HEAD_DIM_X
    outs = []
    for b in range(group):
        qb = qx[b * t_new:(b + 1) * t_new, :]
        qrows = jnp.where(own, jnp.concatenate([qb] * N_HEADS_X, axis=0), 0.0).astype(BF16)
        p, inv = _softmax_rows(_mm_nt(qrows, mk_ref[b].astype(BF16)))
        o = _mm(p, mv_ref[b].astype(BF16)) * inv
        out = jnp.zeros((t_new, D_MODEL), F32)
        for h in range(N_HEADS_X):
            out = jnp.where(lane_head == h, o[h * t_new:(h + 1) * t_new, :], out)
        outs.append(out)
    y_ref[...] = x2 + _mm(jnp.concatenate(outs, axis=0).astype(BF16), wo_ref[...])


def _mix_sample(x, uext, oa, mk, mv, cw, cb, lg, lb, wout, gx, wq, wo, batch, t_new, t_ext, group):
    rows = group * t_new
    row = lambda w: pl.BlockSpec((rows, w), lambda i: (i, 0))
    ext = pl.BlockSpec((group * t_ext + HALO, WIDTH_B), lambda i: (i, 0))
    mem = pl.BlockSpec((group, N_MEM, D_MODEL), lambda i: (i, 0, 0))
    vec = lambda w: _const_spec((1, w))
    return pl.pallas_call(
        functools.partial(_mix_sample_kernel, group=group, t_new=t_new, t_ext=t_ext),
        grid=(batch // group,),
        in_specs=[row(D_MODEL), ext, row(WIDTH_A), mem, mem, _const_spec(cw.shape),
                  vec(WIDTH_B), vec(WIDTH_B), vec(WIDTH_B), _const_spec(wout.shape), vec(D_MODEL),
                  _const_spec(wq.shape), _const_spec(wo.shape)],
        out_specs=row(D_MODEL),
        out_shape=jax.ShapeDtypeStruct((batch * t_new, D_MODEL), F32),
        scratch_shapes=[pltpu.VMEM((SUBLANES - 1, _shifted_rows(0, group * t_ext), WIDTH_B), F32)],
        compiler_params=_params("parallel"),
        name="mix_sample",
    )(x, uext, oa, mk, mv, cw, cb, lg, lb, wout, gx, wq, wo)


def kernel(x_prompt, x_sample, mem_prompt, cache_win_k, cache_win_v, cache_conv, cache_mem_k, cache_mem_v, ffn1_norm, ffn1_gate, ffn1_up, ffn1_down, mix_norm, w_in, conv_w, conv_b, conv_ln_g, conv_ln_b, w_out, xattn_norm, mem_norm, w_cq, w_ck, w_cv, w_co, ffn2_norm, ffn2_gate, ffn2_up, ffn2_down, final_norm):
    depth = ffn1_norm.shape[0]
    assert depth == 1
    bp, seq, _ = x_prompt.shape
    bs, t_new, _ = x_sample.shape
    w_buf = cache_win_k.shape[2]
    keep = CONV_WIDTH - 1
    l = 0
    vec = lambda a: a.reshape(1, -1)
    bf = lambda a: a.astype(BF16)
    slopes = jnp.asarray([2.0 ** -(h + 1) for h in range(N_HEADS_A)], F32)

    f1 = (vec(ffn1_norm[l]), bf(ffn1_gate[l]), bf(ffn1_up[l]), bf(ffn1_down[l]))
    f2 = (vec(ffn2_norm[l]), bf(ffn2_gate[l]), bf(ffn2_up[l]), bf(ffn2_down[l]))
    gmix, win = vec(mix_norm[l]), bf(w_in[l])
    conv = (conv_w[l], vec(conv_b[l]), vec(conv_ln_g[l]), vec(conv_ln_b[l]))
    proj = (bf(w_out[l]), vec(xattn_norm[l]), bf(w_cq[l]), bf(w_co[l]))
    gfin = vec(final_norm)

    xp = x_prompt.reshape(bp * seq, D_MODEL)
    x1, q, k, v, u = _ffn_proj(xp, *f1, gmix, win, tm=512)
    oa = _attn_prompt(q, k, v, slopes, bp, seq)
    mk, mv = _mem_kv(mem_prompt.reshape(bp * N_MEM, D_MODEL), vec(mem_norm[l]), bf(w_ck[l]), bf(w_cv[l]), tm=512)
    x3 = _mix_prompt(x1, u, oa, mk, mv, *conv, *proj, batch=bp, seq=seq, ts=512)
    yp = _ffn_final(x3, *f2, gfin, tm=512)

    xs = x_sample.reshape(bs * t_new, D_MODEL)
    s1, sq, sk, sv, su = _ffn_proj(xs, *f1, gmix, win, tm=bs * t_new)
    soa = _attn_sample(sq, sk, sv, cache_win_k[l].reshape(bs, w_buf, WIDTH_A),
                       cache_win_v[l].reshape(bs, w_buf, WIDTH_A), t_new)
    u_ext = jnp.concatenate([cache_conv[l], su.reshape(bs, t_new, WIDTH_B)], axis=1)
    t_ext = -(-(keep + t_new) // 8) * 8
    group = 8
    u_pad = jnp.pad(u_ext, ((0, 0), (0, t_ext - keep - t_new), (0, 0))).reshape(bs // group, group * t_ext, WIDTH_B)
    u_pad = jnp.pad(u_pad, ((0, 0), (0, HALO), (0, 0))).reshape(-1, WIDTH_B)
    s3 = _mix_sample(s1, u_pad, soa, cache_mem_k[l].reshape(bs, N_MEM, D_MODEL),
                     cache_mem_v[l].reshape(bs, N_MEM, D_MODEL), *conv, *proj,
                     batch=bs, t_new=t_new, t_ext=t_ext, group=group)
    ys = _ffn_final(s3, *f2, gfin, tm=bs * t_new)

    heads = lambda a, b, t: a.reshape(1, b, t, N_HEADS_A, HEAD_DIM)
    return (yp.reshape(bp, seq, D_MODEL),
            ys.reshape(bs, t_new, D_MODEL),
            heads(k, bp, seq), heads(v, bp, seq),
            u.reshape(1, bp, seq, WIDTH_B)[:, :, seq - keep:],
            mk.reshape(1, bp, N_MEM, N_HEADS_X, HEAD_DIM_X), mv.reshape(1, bp, N_MEM, N_HEADS_X, HEAD_DIM_X),
            heads(sk, bs, t_new), heads(sv, bs, t_new),
            u_ext[None, :, t_new:])
```

```python
import functools

import jax
import jax.numpy as jnp
from jax import lax
from jax.experimental import pallas as pl
from jax.experimental.pallas import tpu as pltpu

D_MODEL = 1024
HEAD_DIM = 64
N_HEADS_A = 8
WIDTH_A = N_HEADS_A * HEAD_DIM
WIDTH_B = D_MODEL - WIDTH_A
DILATED_BRANCHES = ((128, 1), (512, 4), (2048, 16))
BAND = 128
CONV_WIDTH = 31
D_FF = 2816
N_MEM = 256
N_HEADS_X = 4
HEAD_DIM_X = D_MODEL // N_HEADS_X
EPS = 1e-6

LANES = 128
SUBLANES = 8
FF_CHUNK = 256
HALO = 32
MIX_ROWS = 512
NEG_BIG = -1e30
VMEM_LIMIT = 56 * 1024 * 1024

F32 = jnp.float32
BF16 = jnp.bfloat16


def _const_spec(shape):
    nd = len(shape)
    return pl.BlockSpec(shape, lambda *_: (0,) * nd, pipeline_mode=pl.Buffered(1))


def _params(*sem):
    return pltpu.CompilerParams(dimension_semantics=sem, vmem_limit_bytes=VMEM_LIMIT)


def _rms(x, g):
    return x * lax.rsqrt(jnp.mean(x * x, axis=-1, keepdims=True) + EPS) * g


def _mm(a, b):
    return jnp.dot(a, b, preferred_element_type=F32)


def _mm_nt(a, b):
    return lax.dot_general(a, b, (((1,), (1,)), ((), ())), preferred_element_type=F32)


def _swiglu_half_step(x, g_ref, wg_ref, wu_ref, wd_ref, act_ref):
    h = _rms(x, g_ref[...]).astype(BF16)
    for c in range(0, D_FF, FF_CHUNK):
        gate = _mm(h, wg_ref[:, c:c + FF_CHUNK])
        up = _mm(h, wu_ref[:, c:c + FF_CHUNK])
        act_ref[:, c:c + FF_CHUNK] = (gate * jax.nn.sigmoid(gate) * up).astype(BF16)
    return x + 0.5 * _mm(act_ref[...], wd_ref[...])


def _ffn_proj_kernel(x_ref, g_ref, wg_ref, wu_ref, wd_ref, gmix_ref, win_ref,
                     x1_ref, q_ref, k_ref, v_ref, u_ref, kh_ref, vh_ref, act_ref):
    x1 = _swiglu_half_step(x_ref[...], g_ref, wg_ref, wu_ref, wd_ref, act_ref)
    x1_ref[...] = x1
    h = _rms(x1, gmix_ref[...]).astype(BF16)
    w = WIDTH_A
    q_ref[...] = _mm(h, win_ref[:, 0:w]) * (HEAD_DIM ** -0.5)
    k = _mm(h, win_ref[:, w:2 * w])
    v = _mm(h, win_ref[:, 2 * w:3 * w])
    k_ref[...] = k
    v_ref[...] = v
    kh_ref[...] = k.reshape(kh_ref.shape)
    vh_ref[...] = v.reshape(vh_ref.shape)
    a = _mm(h, win_ref[:, 3 * w:3 * w + WIDTH_B])
    g = _mm(h, win_ref[:, 3 * w + WIDTH_B:3 * w + 2 * WIDTH_B])
    u_ref[...] = a * jax.nn.sigmoid(g)


def _ffn_proj(x, g, wg, wu, wd, gmix, win, tm):
    n = x.shape[0]
    row = lambda w: pl.BlockSpec((tm, w), lambda i: (i, 0))
    heads = pl.BlockSpec((tm, N_HEADS_A, HEAD_DIM), lambda i: (i, 0, 0))
    return pl.pallas_call(
        _ffn_proj_kernel,
        grid=(n // tm,),
        in_specs=[row(D_MODEL), _const_spec((1, D_MODEL)), _const_spec(wg.shape), _const_spec(wu.shape),
                  _const_spec(wd.shape), _const_spec((1, D_MODEL)), _const_spec(win.shape)],
        out_specs=[row(D_MODEL), row(WIDTH_A), row(WIDTH_A), row(WIDTH_A), row(WIDTH_B), heads, heads],
        out_shape=[jax.ShapeDtypeStruct((n, D_MODEL), F32)] + [jax.ShapeDtypeStruct((n, WIDTH_A), F32)] * 4
        + [jax.ShapeDtypeStruct((n, N_HEADS_A, HEAD_DIM), F32)] * 2,
        scratch_shapes=[pltpu.VMEM((tm, D_FF), BF16)],
        compiler_params=_params("parallel"),
        name="ffn_proj",
    )(x, g, wg, wu, wd, gmix, win)


def _ffn_final_kernel(x_ref, g_ref, wg_ref, wu_ref, wd_ref, gfin_ref, y_ref, act_ref):
    x1 = _swiglu_half_step(x_ref[...], g_ref, wg_ref, wu_ref, wd_ref, act_ref)
    y_ref[...] = _rms(x1, gfin_ref[...])


def _ffn_final(x, g, wg, wu, wd, gfin, tm):
    n = x.shape[0]
    row = pl.BlockSpec((tm, D_MODEL), lambda i: (i, 0))
    return pl.pallas_call(
        _ffn_final_kernel,
        grid=(n // tm,),
        in_specs=[row, _const_spec((1, D_MODEL)), _const_spec(wg.shape), _const_spec(wu.shape),
                  _const_spec(wd.shape), _const_spec((1, D_MODEL))],
        out_specs=row,
        out_shape=jax.ShapeDtypeStruct((n, D_MODEL), F32),
        scratch_shapes=[pltpu.VMEM((tm, D_FF), BF16)],
        compiler_params=_params("parallel"),
        name="ffn_final",
    )(x, g, wg, wu, wd, gfin)


def _attn_prompt_kernel(slopes_ref, q_ref, k_ref, v_ref, o_ref,
                        x4_ref, qs_ref, kp_ref, vp_ref, bias_ref, s_ref, p_ref, m_ref, l_ref, n_ref,
                        fm_ref, fl_ref, fn_ref, *, seq):
    pair = pl.program_id(1)
    nb = seq // BAND
    lo = lax.broadcasted_iota(jnp.int32, (BAND, LANES), 1) < HEAD_DIM

    @pl.when(pair == 0)
    def _():
        kp_ref[:, 0:BAND, :] = jnp.zeros((3, BAND, LANES), BF16)
        vp_ref[:, 0:BAND, 0:LANES] = jnp.zeros((3, BAND, LANES), BF16)
        vp_ref[:, :, LANES:2 * LANES] = jnp.ones((3, BAND + seq, LANES), BF16)
        p_ref[...] = jnp.zeros(p_ref.shape, BF16)

    def put(kind, b, first, x):
        nblocks = x.shape[0] // BAND
        if kind == 0:
            is_lo = lax.broadcasted_iota(jnp.int32, x.shape, 1) < HEAD_DIM
            x_lo = jnp.where(is_lo, x, 0.0).astype(BF16)
            x_hi = jnp.where(is_lo, 0.0, x).astype(BF16)
            for j in range(nblocks):
                qs_ref[b, first + j, 0] = x_lo[j * BAND:(j + 1) * BAND]
                qs_ref[b, first + j, 1] = x_hi[j * BAND:(j + 1) * BAND]
        elif kind == 1:
            kp_ref[b, BAND + first * BAND:BAND + first * BAND + x.shape[0], :] = x.astype(BF16)
        else:
            vp_ref[b, BAND + first * BAND:BAND + first * BAND + x.shape[0], 0:LANES] = x.astype(BF16)

    quarter = seq // 4
    for kind, ref in enumerate((q_ref, k_ref, v_ref)):
        for r in range(4):
            put(kind, 0, 4 * r, ref[r * quarter:(r + 1) * quarter, :])
            x = ref[pl.ds(r, quarter, stride=4), :]
            x4_ref[r * quarter:(r + 1) * quarter, :] = x
            put(kind, 1, 4 * r, x)
        for r in range(16):
            put(kind, 2, r, x4_ref[pl.ds((r % 4) * quarter + r // 4, seq // 16, stride=4), :])

    qi = lax.broadcasted_iota(jnp.int32, (BAND, 2 * BAND), 0)
    kj = lax.broadcasted_iota(jnp.int32, (BAND, 2 * BAND), 1)
    dist = qi + BAND - kj
    neg_dist = jnp.where((dist >= 0) & (dist <= BAND), -dist.astype(F32), NEG_BIG)
    neg_dist_cur = jnp.where(kj >= BAND, neg_dist, NEG_BIG)
    for b, (_, dil) in enumerate(DILATED_BRANCHES):
        for noprev in range(2):
            if 2 * b + noprev < bias_ref.shape[0]:
                table = neg_dist_cur if noprev else neg_dist
                for half in range(2):
                    bias_ref[2 * b + noprev, half * BAND:(half + 1) * BAND, :] = (
                        (table * slopes_ref[2 * pair + half]) * float(dil))

    def blocks_per_class(b):
        return seq // DILATED_BRANCHES[b][1] // BAND

    def scores(n):
        out = []
        for b in range(3):
            qb = qs_ref[b, n].reshape(2 * BAND, LANES)
            if blocks_per_class(b) > 1:
                kb = kp_ref[b, pl.ds(pl.multiple_of(n * BAND, BAND), 2 * BAND), :]
                bias = bias_ref[2 * b + (n % blocks_per_class(b) == 0).astype(jnp.int32)]
            else:
                kb = kp_ref[b, pl.ds(pl.multiple_of(n * BAND + BAND, BAND), BAND), :]
                bias = bias_ref[2 * b, :, BAND:2 * BAND]
            out.append(_mm_nt(qb, kb) + bias)
        return out

    def keys_of(b):
        return 2 * BAND if blocks_per_class(b) > 1 else BAND

    def softmax(n):
        rows = pl.ds(pl.multiple_of(n * BAND, BAND), BAND)
        for b in range(3):
            s = s_ref[b, :, 0:keys_of(b)]
            m = jnp.max(s, axis=-1, keepdims=True)
            p_ref[b, :, 0:keys_of(b)] = jnp.exp(s - m).astype(BF16)
            m_ref[b, rows, :] = jnp.where(lo, m[0:BAND], m[BAND:2 * BAND])

    def weighted_values(n):
        out = []
        for b in range(3):
            first = n * BAND + (2 * BAND - keys_of(b))
            vb = vp_ref[b, pl.ds(pl.multiple_of(first, BAND), keys_of(b)), :]
            out.append(_mm(p_ref[b, :, 0:keys_of(b)], vb))
        return out

    def step(n, carry):
        s_next = scores(jnp.minimum(n + 1, nb - 1))
        done = jnp.maximum(n - 1, 0)
        pv = weighted_values(done)
        softmax(jnp.minimum(n, nb - 1))
        rows = pl.ds(pl.multiple_of(done * BAND, BAND), BAND)
        for b in range(3):
            s_ref[b, :, 0:keys_of(b)] = s_next[b]
            n_ref[b, rows, :] = jnp.where(lo, pv[b][0:BAND, 0:LANES], pv[b][BAND:2 * BAND, 0:LANES])
            l_ref[b, rows, :] = jnp.where(lo, pv[b][0:BAND, LANES:2 * LANES], pv[b][BAND:2 * BAND, LANES:2 * LANES])
        return carry

    for b, s0 in enumerate(scores(jnp.int32(0))):
        s_ref[b, :, 0:keys_of(b)] = s0
    lax.fori_loop(0, nb + 1, step, 0)

    per16 = seq // 16
    for r in range(16):
        sl4 = pl.ds((r % 4) * quarter + r // 4, per16, stride=4)
        blk = pl.ds(r * per16, per16)
        m1, m2 = m_ref[1, sl4, :], m_ref[2, blk, :]
        mx = jnp.maximum(m1, m2)
        e1, e2 = jnp.exp(m1 - mx), jnp.exp(m2 - mx)
        fm_ref[sl4, :] = mx
        fl_ref[sl4, :] = e1 * l_ref[1, sl4, :] + e2 * l_ref[2, blk, :]
        fn_ref[sl4, :] = e1 * n_ref[1, sl4, :] + e2 * n_ref[2, blk, :]
    for r in range(4):
        for c in range(quarter // BAND):
            sl = pl.ds(r + 4 * BAND * c, BAND, stride=4)
            blk = pl.ds(r * quarter + c * BAND, BAND)
            m0, m1 = m_ref[0, sl, :], fm_ref[blk, :]
            mx = jnp.maximum(m0, m1)
            e0, e1 = jnp.exp(m0 - mx), jnp.exp(m1 - mx)
            den = e0 * l_ref[0, sl, :] + e1 * fl_ref[blk, :]
            num = e0 * n_ref[0, sl, :] + e1 * fn_ref[blk, :]
            o_ref[sl, :] = num / den


def _attn_prompt(q, k, v, slopes, batch, seq):
    assert [seq // d // BAND for _, d in DILATED_BRANCHES] == [16, 4, 1]
    nb = seq // BAND
    blk = pl.BlockSpec((seq, LANES), lambda b, p: (b, p))
    return pl.pallas_call(
        functools.partial(_attn_prompt_kernel, seq=seq),
        grid=(batch, WIDTH_A // LANES),
        in_specs=[pl.BlockSpec(memory_space=pltpu.SMEM), blk, blk, blk],
        out_specs=blk,
        out_shape=jax.ShapeDtypeStruct((batch * seq, WIDTH_A), F32),
        scratch_shapes=[pltpu.VMEM((seq, LANES), F32),
                        pltpu.VMEM((3, nb, 2, BAND, LANES), BF16),
                        pltpu.VMEM((3, BAND + seq, LANES), BF16),
                        pltpu.VMEM((3, BAND + seq, 2 * LANES), BF16),
                        pltpu.VMEM((5, 2 * BAND, 2 * BAND), F32),
                        pltpu.VMEM((3, 2 * BAND, 2 * BAND), F32),
                        pltpu.VMEM((3, 2 * BAND, 2 * BAND), BF16)]
        + [pltpu.VMEM((3, seq, LANES), F32)] * 3 + [pltpu.VMEM((seq, LANES), F32)] * 3,
        compiler_params=_params("parallel", "arbitrary"),
        name="attn_prompt",
    )(slopes, q, k, v)


def _attn_sample_kernel(q_ref, kn_ref, vn_ref, kfar_ref, knear_ref, vfar_ref, vnear_ref, o_ref, kall_ref, vall_ref,
                        *, t_new, w_buf, far_groups, near, pad):
    rows = N_HEADS_A * t_new
    n_far = far_groups * t_new
    stride = DILATED_BRANCHES[-1][1]
    flat = lambda ref, n: ref[...].reshape(n, N_HEADS_A, HEAD_DIM).reshape(n, WIDTH_A).astype(BF16)
    kall_ref[0:n_far, :] = flat(kfar_ref, n_far)
    vall_ref[0:n_far, :] = flat(vfar_ref, n_far)
    kall_ref[n_far:n_far + near, :] = flat(knear_ref, near)
    vall_ref[n_far:n_far + near, :] = flat(vnear_ref, near)
    zeros = jnp.zeros((pad - t_new, WIDTH_A), F32)
    kall_ref[n_far + near:n_far + near + pad, :] = jnp.concatenate([kn_ref[...], zeros], axis=0).astype(BF16)
    vall_ref[n_far + near:n_far + near + pad, :] = jnp.concatenate([vn_ref[...], zeros], axis=0).astype(BF16)

    qrep = jnp.concatenate([q_ref[...]] * N_HEADS_A, axis=0)
    rr = lax.broadcasted_iota(jnp.int32, (rows, WIDTH_A), 0)
    ll = lax.broadcasted_iota(jnp.int32, (rows, WIDTH_A), 1)
    qrows = jnp.where(rr // t_new == ll // HEAD_DIM, qrep, 0.0).astype(BF16)
    s = _mm_nt(qrows, kall_ref[...])

    r2 = lax.broadcasted_iota(jnp.int32, s.shape, 0)
    c2 = lax.broadcasted_iota(jnp.int32, s.shape, 1)
    u2 = jnp.where(c2 < n_far, c2 // t_new * stride + c2 % t_new, c2 + (w_buf - near - n_far))
    d = w_buf + r2 % t_new - u2
    mult = jnp.zeros(s.shape, F32)
    for window, dil in DILATED_BRANCHES:
        mult = mult + jnp.where((d >= 0) & (d <= window) & (d % dil == 0), 1.0, 0.0)
    head = lax.broadcasted_iota(jnp.int32, (rows, 1), 0) // t_new
    slope = jnp.zeros((rows, 1), F32)
    for h in range(N_HEADS_A):
        slope = jnp.where(head == h, 2.0 ** -(h + 1), slope)
    s = jnp.where(mult > 0, s - d.astype(F32) * slope, NEG_BIG)
    m = jnp.max(s, axis=-1, keepdims=True)
    p = mult * jnp.exp(s - m)
    den = jnp.sum(p, axis=-1, keepdims=True)
    o = _mm(p.astype(BF16), vall_ref[...]) * (1.0 / den)

    lane_head = lax.broadcasted_iota(jnp.int32, (t_new, WIDTH_A), 1) // HEAD_DIM
    out = jnp.zeros((t_new, WIDTH_A), F32)
    for h in range(N_HEADS_A):
        out = jnp.where(lane_head == h, o[h * t_new:(h + 1) * t_new, :], out)
    o_ref[...] = out


def _attn_sample(q, k, v, cache_k, cache_v, t_new):
    batch, w_buf = cache_k.shape[:2]
    stride = DILATED_BRANCHES[-1][1]
    near = DILATED_BRANCHES[-2][0]
    assert t_new % SUBLANES == 0 and t_new <= stride and w_buf % stride == 0
    assert w_buf >= DILATED_BRANCHES[-1][0] and (w_buf - near) % stride == 0 and w_buf % near == 0
    far_groups = (w_buf - near) // stride
    pad = LANES
    new = pl.BlockSpec((t_new, WIDTH_A), lambda b: (b, 0))
    far = pl.BlockSpec((None, far_groups, t_new, N_HEADS_A, HEAD_DIM), lambda b: (b, 0, 0, 0, 0))
    near_spec = pl.BlockSpec((None, near, N_HEADS_A, HEAD_DIM), lambda b: (b, w_buf // near - 1, 0, 0))
    grouped = lambda c: c.reshape(batch, w_buf // stride, stride, N_HEADS_A, HEAD_DIM)
    n_keys = far_groups * t_new + near + pad
    return pl.pallas_call(
        functools.partial(_attn_sample_kernel, t_new=t_new, w_buf=w_buf, far_groups=far_groups, near=near, pad=pad),
        grid=(batch,),
        in_specs=[new, new, new, far, near_spec, far, near_spec],
        out_specs=new,
        out_shape=jax.ShapeDtypeStruct((batch * t_new, WIDTH_A), F32),
        scratch_shapes=[pltpu.VMEM((n_keys, WIDTH_A), BF16)] * 2,
        compiler_params=_params("parallel"),
        name="attn_sample",
    )(q, k, v, grouped(cache_k), cache_k, grouped(cache_v), cache_v)


def _mem_kv_kernel(mem_ref, g_ref, wk_ref, wv_ref, mk_ref, mv_ref, mkh_ref, mvh_ref):
    h = _rms(mem_ref[...], g_ref[...]).astype(BF16)
    mk = _mm(h, wk_ref[...])
    mv = _mm(h, wv_ref[...])
    mk_ref[...] = mk.astype(BF16)
    mv_ref[...] = mv.astype(BF16)
    mkh_ref[...] = mk.reshape(mkh_ref.shape)
    mvh_ref[...] = mv.reshape(mvh_ref.shape)


def _mem_kv(mem, g, wk, wv, tm):
    n = mem.shape[0]
    row = pl.BlockSpec((tm, D_MODEL), lambda i: (i, 0))
    heads = pl.BlockSpec((tm, N_HEADS_X, HEAD_DIM_X), lambda i: (i, 0, 0))
    return pl.pallas_call(
        _mem_kv_kernel,
        grid=(n // tm,),
        in_specs=[row, _const_spec((1, D_MODEL)), _const_spec(wk.shape), _const_spec(wv.shape)],
        out_specs=[row, row, heads, heads],
        out_shape=[jax.ShapeDtypeStruct((n, D_MODEL), BF16)] * 2
        + [jax.ShapeDtypeStruct((n, N_HEADS_X, HEAD_DIM_X), F32)] * 2,
        compiler_params=_params("parallel"),
        name="mem_kv",
    )(mem, g, wk, wv)


def _shifted_rows(first_row, rows):
    offs = [first_row + j for j in range(CONV_WIDTH) if (first_row + j) % SUBLANES]
    return max(offs) // SUBLANES * SUBLANES + rows


def _realign_conv_input(ext_ref, sh_ref):
    for s in range(1, SUBLANES):
        sh_ref[s - 1] = ext_ref[pl.ds(s, sh_ref.shape[1]), :]


def _conv_module(ext_ref, sh_ref, first_row, row0, rows, cw_ref, cb_ref, lg_ref, lb_ref):
    y = cb_ref[...]
    for j in range(CONV_WIDTH):
        base, s = (first_row + j) // SUBLANES * SUBLANES, (first_row + j) % SUBLANES
        at = pl.ds(base + row0, rows)
        y = y + cw_ref[j:j + 1, :] * (ext_ref[at, :] if s == 0 else sh_ref[s - 1, at, :])
    yc = y - jnp.mean(y, axis=-1, keepdims=True)
    yn = yc * lax.rsqrt(jnp.mean(yc * yc, axis=-1, keepdims=True) + EPS) * lg_ref[...] + lb_ref[...]
    return yn * jax.nn.sigmoid(yn)


def _softmax_rows(s):
    m = jnp.max(s, axis=-1, keepdims=True)
    p = jnp.exp(s - m)
    return p.astype(BF16), 1.0 / jnp.sum(p, axis=-1, keepdims=True)


def _mix_prompt_kernel(x_ref, u_ref, halo_ref, oa_ref, mk_ref, mv_ref, cw_ref, cb_ref, lg_ref, lb_ref,
                       wout_ref, gx_ref, wq_ref, wo_ref, y_ref, ext_ref, sh_ref):
    ts = x_ref.shape[0]
    halo = halo_ref[...]
    ext_ref[0:HALO, :] = jnp.where(pl.program_id(1) == 0, jnp.zeros_like(halo), halo)
    ext_ref[HALO:HALO + ts, :] = u_ref[...]
    _realign_conv_input(ext_ref, sh_ref)
    mk = mk_ref[...]
    mv = mv_ref[...]
    for r0 in range(0, ts, MIX_ROWS):
        rows = slice(r0, r0 + MIX_ROWS)
        ob = _conv_module(ext_ref, sh_ref, HALO - (CONV_WIDTH - 1), r0, MIX_ROWS, cw_ref, cb_ref, lg_ref, lb_ref)
        x2 = (x_ref[rows, :] + _mm(oa_ref[rows, :].astype(BF16), wout_ref[0:WIDTH_A, :])
              + _mm(ob.astype(BF16), wout_ref[WIDTH_A:WIDTH_A + WIDTH_B, :]))
        hx = _rms(x2, gx_ref[...]).astype(BF16)
        qx = (_mm(hx, wq_ref[...]) * (HEAD_DIM_X ** -0.5)).astype(BF16)
        outs = []
        for h in range(N_HEADS_X):
            sl = slice(h * HEAD_DIM_X, (h + 1) * HEAD_DIM_X)
            p, inv = _softmax_rows(_mm_nt(qx[:, sl], mk[:, sl]))
            outs.append((_mm(p, mv[:, sl]) * inv).astype(BF16))
        y_ref[rows, :] = x2 + _mm(jnp.concatenate(outs, axis=-1), wo_ref[...])


def _mix_prompt(x, u, oa, mk, mv, cw, cb, lg, lb, wout, gx, wq, wo, batch, seq, ts):
    tiles = seq // ts
    row = lambda w: pl.BlockSpec((ts, w), lambda b, i: (b * tiles + i, 0))
    halo = pl.BlockSpec((HALO, WIDTH_B),
                        lambda b, i: (jnp.maximum((b * seq + i * ts) // HALO - 1, 0), 0))
    mem = pl.BlockSpec((N_MEM, D_MODEL), lambda b, i: (b, 0))
    vec = lambda w: _const_spec((1, w))
    return pl.pallas_call(
        _mix_prompt_kernel,
        grid=(batch, tiles),
        in_specs=[row(D_MODEL), row(WIDTH_B), halo, row(WIDTH_A), mem, mem, _const_spec(cw.shape),
                  vec(WIDTH_B), vec(WIDTH_B), vec(WIDTH_B), _const_spec(wout.shape), vec(D_MODEL),
                  _const_spec(wq.shape), _const_spec(wo.shape)],
        out_specs=row(D_MODEL),
        out_shape=jax.ShapeDtypeStruct((batch * seq, D_MODEL), F32),
        scratch_shapes=[pltpu.VMEM((HALO + ts, WIDTH_B), F32),
                        pltpu.VMEM((SUBLANES - 1, _shifted_rows(HALO - (CONV_WIDTH - 1), ts), WIDTH_B), F32)],
        compiler_params=_params("parallel", "parallel"),
        name="mix_prompt",
    )(x, u, u, oa, mk, mv, cw, cb, lg, lb, wout, gx, wq, wo)


def _mix_sample_kernel(x_ref, uext_ref, oa_ref, mk_ref, mv_ref, cw_ref, cb_ref, lg_ref, lb_ref,
                       wout_ref, gx_ref, wq_ref, wo_ref, y_ref, sh_ref, *, group, t_new, t_ext):
    rows = group * t_ext
    _realign_conv_input(uext_ref, sh_ref)
    conv = _conv_module(uext_ref, sh_ref, 0, 0, rows, cw_ref, cb_ref, lg_ref, lb_ref)
    ob = conv.reshape(group, t_ext, WIDTH_B)[:, 0:t_new, :].reshape(group * t_new, WIDTH_B)
    x2 = (x_ref[...] + _mm(oa_ref[...].astype(BF16), wout_ref[0:WIDTH_A, :])
          + _mm(ob.astype(BF16), wout_ref[WIDTH_A:WIDTH_A + WIDTH_B, :]))

    hx = _rms(x2, gx_ref[...]).astype(BF16)
    qx = _mm(hx, wq_ref[...]) * (HEAD_DIM_X ** -0.5)
    qrows_n = N_HEADS_X * t_new
    rr = lax.broadcasted_iota(jnp.int32, (qrows_n, D_MODEL), 0)
    ll = lax.broadcasted_iota(jnp.int32, (qrows_n, D_MODEL), 1)
    own = rr // t_new == ll // HEAD_DIM_X
    lane_head = lax.broadcasted_iota(jnp.int32, (t_new, D_MODEL), 1) // HEAD_DIM_X
    outs = []
    for b in range(group):
        qb = qx[b * t_new:(b + 1) * t_new, :]
        qrows = jnp.where(own, jnp.concatenate([qb] * N_HEADS_X, axis=0), 0.0).astype(BF16)
        mk = mk_ref[b].reshape(N_MEM, D_MODEL).astype(BF16)
        mv = mv_ref[b].reshape(N_MEM, D_MODEL).astype(BF16)
        p, inv = _softmax_rows(_mm_nt(qrows, mk))
        o = _mm(p, mv) * inv
        out = jnp.zeros((t_new, D_MODEL), F32)
        for h in range(N_HEADS_X):
            out = jnp.where(lane_head == h, o[h * t_new:(h + 1) * t_new, :], out)
        outs.append(out)
    y_ref[...] = x2 + _mm(jnp.concatenate(outs, axis=0).astype(BF16), wo_ref[...])


def _mix_sample(x, uext, oa, mk, mv, cw, cb, lg, lb, wout, gx, wq, wo, batch, t_new, t_ext, group):
    rows = group * t_new
    row = lambda w: pl.BlockSpec((rows, w), lambda i: (i, 0))
    ext = pl.BlockSpec((group * t_ext + HALO, WIDTH_B), lambda i: (i, 0))
    mem = pl.BlockSpec((group, N_MEM, N_HEADS_X, HEAD_DIM_X), lambda i: (i, 0, 0, 0))
    vec = lambda w: _const_spec((1, w))
    return pl.pallas_call(
        functools.partial(_mix_sample_kernel, group=group, t_new=t_new, t_ext=t_ext),
        grid=(batch // group,),
        in_specs=[row(D_MODEL), ext, row(WIDTH_A), mem, mem, _const_spec(cw.shape),
                  vec(WIDTH_B), vec(WIDTH_B), vec(WIDTH_B), _const_spec(wout.shape), vec(D_MODEL),
                  _const_spec(wq.shape), _const_spec(wo.shape)],
        out_specs=row(D_MODEL),
        out_shape=jax.ShapeDtypeStruct((batch * t_new, D_MODEL), F32),
        scratch_shapes=[pltpu.VMEM((SUBLANES - 1, _shifted_rows(0, group * t_ext), WIDTH_B), F32)],
        compiler_params=_params("parallel"),
        name="mix_sample",
    )(x, uext, oa, mk, mv, cw, cb, lg, lb, wout, gx, wq, wo)


def kernel(x_prompt, x_sample, mem_prompt, cache_win_k, cache_win_v, cache_conv, cache_mem_k, cache_mem_v, ffn1_norm, ffn1_gate, ffn1_up, ffn1_down, mix_norm, w_in, conv_w, conv_b, conv_ln_g, conv_ln_b, w_out, xattn_norm, mem_norm, w_cq, w_ck, w_cv, w_co, ffn2_norm, ffn2_gate, ffn2_up, ffn2_down, final_norm):
    depth = ffn1_norm.shape[0]
    assert depth == 1
    bp, seq, _ = x_prompt.shape
    bs, t_new, _ = x_sample.shape
    keep = CONV_WIDTH - 1
    l = 0
    vec = lambda a: a.reshape(1, -1)
    bf = lambda a: a.astype(BF16)
    slopes = jnp.asarray([2.0 ** -(h + 1) for h in range(N_HEADS_A)], F32)

    f1 = (vec(ffn1_norm[l]), bf(ffn1_gate[l]), bf(ffn1_up[l]), bf(ffn1_down[l]))
    f2 = (vec(ffn2_norm[l]), bf(ffn2_gate[l]), bf(ffn2_up[l]), bf(ffn2_down[l]))
    gmix, win = vec(mix_norm[l]), bf(w_in[l])
    conv = (conv_w[l], vec(conv_b[l]), vec(conv_ln_g[l]), vec(conv_ln_b[l]))
    proj = (bf(w_out[l]), vec(xattn_norm[l]), bf(w_cq[l]), bf(w_co[l]))
    gfin = vec(final_norm)

    xp = x_prompt.reshape(bp * seq, D_MODEL)
    x1, q, k, v, u, kh, vh = _ffn_proj(xp, *f1, gmix, win, tm=512)
    oa = _attn_prompt(q, k, v, slopes, bp, seq)
    mk, mv, mkh, mvh = _mem_kv(mem_prompt.reshape(bp * N_MEM, D_MODEL), vec(mem_norm[l]), bf(w_ck[l]), bf(w_cv[l]),
                               tm=512)
    x3 = _mix_prompt(x1, u, oa, mk, mv, *conv, *proj, batch=bp, seq=seq, ts=512)
    yp = _ffn_final(x3, *f2, gfin, tm=512)

    xs = x_sample.reshape(bs * t_new, D_MODEL)
    s1, sq, sk, sv, su, skh, svh = _ffn_proj(xs, *f1, gmix, win, tm=bs * t_new)
    soa = _attn_sample(sq, sk, sv, cache_win_k[l], cache_win_v[l], t_new)
    u_ext = jnp.concatenate([cache_conv[l], su.reshape(bs, t_new, WIDTH_B)], axis=1)
    t_ext = -(-(keep + t_new) // 8) * 8
    group = 8
    u_pad = jnp.pad(u_ext, ((0, 0), (0, t_ext - keep - t_new), (0, 0))).reshape(bs // group, group * t_ext, WIDTH_B)
    u_pad = jnp.pad(u_pad, ((0, 0), (0, HALO), (0, 0))).reshape(-1, WIDTH_B)
    s3 = _mix_sample(s1, u_pad, soa, cache_mem_k[l], cache_mem_v[l], *conv, *proj,
                     batch=bs, t_new=t_new, t_ext=t_ext, group=group)
    ys = _ffn_final(s3, *f2, gfin, tm=bs * t_new)

    heads = lambda a, b, t: a.reshape(1, b, t, N_HEADS_A, HEAD_DIM)
    mem_heads = lambda a: a.reshape(1, bp, N_MEM, N_HEADS_X, HEAD_DIM_X)
    return (yp.reshape(bp, seq, D_MODEL),
            ys.reshape(bs, t_new, D_MODEL),
            heads(kh, bp, seq), heads(vh, bp, seq),
            u.reshape(1, bp, seq, WIDTH_B)[:, :, seq - keep:],
            mem_heads(mkh), mem_heads(mvh),
            heads(skh, bs, t_new), heads(svh, bs, t_new),
            u_ext[None, :, t_new:])
```

```python
import functools

import jax
import jax.numpy as jnp
from jax import lax
from jax.experimental import pallas as pl
from jax.experimental.pallas import tpu as pltpu

D_MODEL = 1024
HEAD_DIM = 64
N_HEADS_A = 8
WIDTH_A = N_HEADS_A * HEAD_DIM
WIDTH_B = D_MODEL - WIDTH_A
DILATED_BRANCHES = ((128, 1), (512, 4), (2048, 16))
BAND = 128
CONV_WIDTH = 31
D_FF = 2816
N_MEM = 256
N_HEADS_X = 4
HEAD_DIM_X = D_MODEL // N_HEADS_X
EPS = 1e-6

LANES = 128
SUBLANES = 8
FF_CHUNK = 256
HALO = 32
MIX_ROWS = 512
NEG_BIG = -1e30
VMEM_LIMIT = 56 * 1024 * 1024

F32 = jnp.float32
BF16 = jnp.bfloat16


def _const_spec(shape):
    nd = len(shape)
    return pl.BlockSpec(shape, lambda *_: (0,) * nd, pipeline_mode=pl.Buffered(1))


def _params(*sem):
    return pltpu.CompilerParams(dimension_semantics=sem, vmem_limit_bytes=VMEM_LIMIT)


def _rms(x, g):
    return x * lax.rsqrt(jnp.mean(x * x, axis=-1, keepdims=True) + EPS) * g


def _mm(a, b):
    return jnp.dot(a, b, preferred_element_type=F32)


def _mm_nt(a, b):
    return lax.dot_general(a, b, (((1,), (1,)), ((), ())), preferred_element_type=F32)


def _swiglu_half_step(x, g_ref, wg_ref, wu_ref, wd_ref, act_ref):
    h = _rms(x, g_ref[...]).astype(BF16)
    for c in range(0, D_FF, FF_CHUNK):
        gate = _mm(h, wg_ref[:, c:c + FF_CHUNK])
        up = _mm(h, wu_ref[:, c:c + FF_CHUNK])
        act_ref[:, c:c + FF_CHUNK] = (gate * jax.nn.sigmoid(gate) * up).astype(BF16)
    return x + 0.5 * _mm(act_ref[...], wd_ref[...])


def _ffn_proj_kernel(x_ref, g_ref, wg_ref, wu_ref, wd_ref, gmix_ref, win_ref,
                     x1_ref, q_ref, k_ref, v_ref, u_ref, kh_ref, vh_ref, act_ref):
    x1 = _swiglu_half_step(x_ref[...], g_ref, wg_ref, wu_ref, wd_ref, act_ref)
    x1_ref[...] = x1
    h = _rms(x1, gmix_ref[...]).astype(BF16)
    w = WIDTH_A
    q_ref[...] = _mm(h, win_ref[:, 0:w]) * (HEAD_DIM ** -0.5)
    k = _mm(h, win_ref[:, w:2 * w])
    v = _mm(h, win_ref[:, 2 * w:3 * w])
    k_ref[...] = k
    v_ref[...] = v
    if len(kh_ref.shape) == 2:
        kh_ref[...] = k.T
        vh_ref[...] = v.T
    else:
        kh_ref[...] = k.reshape(kh_ref.shape)
        vh_ref[...] = v.reshape(vh_ref.shape)
    a = _mm(h, win_ref[:, 3 * w:3 * w + WIDTH_B])
    g = _mm(h, win_ref[:, 3 * w + WIDTH_B:3 * w + 2 * WIDTH_B])
    u_ref[...] = a * jax.nn.sigmoid(g)


def _ffn_proj(x, g, wg, wu, wd, gmix, win, tm, seq=None):
    n = x.shape[0]
    row = lambda w: pl.BlockSpec((tm, w), lambda i: (i, 0))
    if seq is None:
        heads = pl.BlockSpec((tm, N_HEADS_A, HEAD_DIM), lambda i: (i, 0, 0))
        heads_shape = jax.ShapeDtypeStruct((n, N_HEADS_A, HEAD_DIM), F32)
    else:
        tiles = seq // tm
        heads = pl.BlockSpec((None, WIDTH_A, tm), lambda i: (i // tiles, 0, i % tiles))
        heads_shape = jax.ShapeDtypeStruct((n // seq, WIDTH_A, seq), F32)
    return pl.pallas_call(
        _ffn_proj_kernel,
        grid=(n // tm,),
        in_specs=[row(D_MODEL), _const_spec((1, D_MODEL)), _const_spec(wg.shape), _const_spec(wu.shape),
                  _const_spec(wd.shape), _const_spec((1, D_MODEL)), _const_spec(win.shape)],
        out_specs=[row(D_MODEL), row(WIDTH_A), row(WIDTH_A), row(WIDTH_A), row(WIDTH_B), heads, heads],
        out_shape=[jax.ShapeDtypeStruct((n, D_MODEL), F32)] + [jax.ShapeDtypeStruct((n, WIDTH_A), F32)] * 4
        + [heads_shape] * 2,
        scratch_shapes=[pltpu.VMEM((tm, D_FF), BF16)],
        compiler_params=_params("parallel"),
        name="ffn_proj",
    )(x, g, wg, wu, wd, gmix, win)


def _ffn_final_kernel(x_ref, g_ref, wg_ref, wu_ref, wd_ref, gfin_ref, y_ref, act_ref):
    x1 = _swiglu_half_step(x_ref[...], g_ref, wg_ref, wu_ref, wd_ref, act_ref)
    y_ref[...] = _rms(x1, gfin_ref[...])


def _ffn_final(x, g, wg, wu, wd, gfin, tm):
    n = x.shape[0]
    row = pl.BlockSpec((tm, D_MODEL), lambda i: (i, 0))
    return pl.pallas_call(
        _ffn_final_kernel,
        grid=(n // tm,),
        in_specs=[row, _const_spec((1, D_MODEL)), _const_spec(wg.shape), _const_spec(wu.shape),
                  _const_spec(wd.shape), _const_spec((1, D_MODEL))],
        out_specs=row,
        out_shape=jax.ShapeDtypeStruct((n, D_MODEL), F32),
        scratch_shapes=[pltpu.VMEM((tm, D_FF), BF16)],
        compiler_params=_params("parallel"),
        name="ffn_final",
    )(x, g, wg, wu, wd, gfin)


def _attn_prompt_kernel(slopes_ref, q_ref, k_ref, v_ref, o_ref,
                        x4_ref, qs_ref, kp_ref, vp_ref, bias_ref, s_ref, p_ref, m_ref, l_ref, n_ref,
                        fm_ref, fl_ref, fn_ref, *, seq):
    pair = pl.program_id(1)
    nb = seq // BAND
    lo = lax.broadcasted_iota(jnp.int32, (BAND, LANES), 1) < HEAD_DIM

    @pl.when(pair == 0)
    def _():
        kp_ref[:, 0:BAND, :] = jnp.zeros((3, BAND, LANES), BF16)
        vp_ref[:, 0:BAND, 0:LANES] = jnp.zeros((3, BAND, LANES), BF16)
        vp_ref[:, :, LANES:2 * LANES] = jnp.ones((3, BAND + seq, LANES), BF16)
        p_ref[...] = jnp.zeros(p_ref.shape, BF16)

    def put(kind, b, first, x):
        nblocks = x.shape[0] // BAND
        if kind == 0:
            is_lo = lax.broadcasted_iota(jnp.int32, x.shape, 1) < HEAD_DIM
            x_lo = jnp.where(is_lo, x, 0.0).astype(BF16)
            x_hi = jnp.where(is_lo, 0.0, x).astype(BF16)
            for j in range(nblocks):
                qs_ref[b, first + j, 0] = x_lo[j * BAND:(j + 1) * BAND]
                qs_ref[b, first + j, 1] = x_hi[j * BAND:(j + 1) * BAND]
        elif kind == 1:
            kp_ref[b, BAND + first * BAND:BAND + first * BAND + x.shape[0], :] = x.astype(BF16)
        else:
            vp_ref[b, BAND + first * BAND:BAND + first * BAND + x.shape[0], 0:LANES] = x.astype(BF16)

    quarter = seq // 4
    for kind, ref in enumerate((q_ref, k_ref, v_ref)):
        for r in range(4):
            put(kind, 0, 4 * r, ref[r * quarter:(r + 1) * quarter, :])
            x = ref[pl.ds(r, quarter, stride=4), :]
            x4_ref[r * quarter:(r + 1) * quarter, :] = x
            put(kind, 1, 4 * r, x)
        for r in range(16):
            put(kind, 2, r, x4_ref[pl.ds((r % 4) * quarter + r // 4, seq // 16, stride=4), :])

    qi = lax.broadcasted_iota(jnp.int32, (BAND, 2 * BAND), 0)
    kj = lax.broadcasted_iota(jnp.int32, (BAND, 2 * BAND), 1)
    dist = qi + BAND - kj
    neg_dist = jnp.where((dist >= 0) & (dist <= BAND), -dist.astype(F32), NEG_BIG)
    neg_dist_cur = jnp.where(kj >= BAND, neg_dist, NEG_BIG)
    for b, (_, dil) in enumerate(DILATED_BRANCHES):
        for noprev in range(2):
            if 2 * b + noprev < bias_ref.shape[0]:
                table = neg_dist_cur if noprev else neg_dist
                for half in range(2):
                    bias_ref[2 * b + noprev, half * BAND:(half + 1) * BAND, :] = (
                        (table * slopes_ref[2 * pair + half]) * float(dil))

    def blocks_per_class(b):
        return seq // DILATED_BRANCHES[b][1] // BAND

    def scores(n):
        out = []
        for b in range(3):
            qb = qs_ref[b, n].reshape(2 * BAND, LANES)
            if blocks_per_class(b) > 1:
                kb = kp_ref[b, pl.ds(pl.multiple_of(n * BAND, BAND), 2 * BAND), :]
                bias = bias_ref[2 * b + (n % blocks_per_class(b) == 0).astype(jnp.int32)]
            else:
                kb = kp_ref[b, pl.ds(pl.multiple_of(n * BAND + BAND, BAND), BAND), :]
                bias = bias_ref[2 * b, :, BAND:2 * BAND]
            out.append(_mm_nt(qb, kb) + bias)
        return out

    def keys_of(b):
        return 2 * BAND if blocks_per_class(b) > 1 else BAND

    def softmax(n):
        rows = pl.ds(pl.multiple_of(n * BAND, BAND), BAND)
        for b in range(3):
            s = s_ref[b, :, 0:keys_of(b)]
            m = jnp.max(s, axis=-1, keepdims=True)
            p_ref[b, :, 0:keys_of(b)] = jnp.exp(s - m).astype(BF16)
            m_ref[b, rows, :] = jnp.where(lo, m[0:BAND], m[BAND:2 * BAND])

    def weighted_values(n):
        out = []
        for b in range(3):
            first = n * BAND + (2 * BAND - keys_of(b))
            vb = vp_ref[b, pl.ds(pl.multiple_of(first, BAND), keys_of(b)), :]
            out.append(_mm(p_ref[b, :, 0:keys_of(b)], vb))
        return out

    def step(n, carry):
        s_next = scores(jnp.minimum(n + 1, nb - 1))
        done = jnp.maximum(n - 1, 0)
        pv = weighted_values(done)
        softmax(jnp.minimum(n, nb - 1))
        rows = pl.ds(pl.multiple_of(done * BAND, BAND), BAND)
        for b in range(3):
            s_ref[b, :, 0:keys_of(b)] = s_next[b]
            n_ref[b, rows, :] = jnp.where(lo, pv[b][0:BAND, 0:LANES], pv[b][BAND:2 * BAND, 0:LANES])
            l_ref[b, rows, :] = jnp.where(lo, pv[b][0:BAND, LANES:2 * LANES], pv[b][BAND:2 * BAND, LANES:2 * LANES])
        return carry

    for b, s0 in enumerate(scores(jnp.int32(0))):
        s_ref[b, :, 0:keys_of(b)] = s0
    lax.fori_loop(0, nb + 1, step, 0)

    per16 = seq // 16
    for r in range(16):
        sl4 = pl.ds((r % 4) * quarter + r // 4, per16, stride=4)
        blk = pl.ds(r * per16, per16)
        m1, m2 = m_ref[1, sl4, :], m_ref[2, blk, :]
        mx = jnp.maximum(m1, m2)
        e1, e2 = jnp.exp(m1 - mx), jnp.exp(m2 - mx)
        fm_ref[sl4, :] = mx
        fl_ref[sl4, :] = e1 * l_ref[1, sl4, :] + e2 * l_ref[2, blk, :]
        fn_ref[sl4, :] = e1 * n_ref[1, sl4, :] + e2 * n_ref[2, blk, :]
    for r in range(4):
        for c in range(quarter // BAND):
            sl = pl.ds(r + 4 * BAND * c, BAND, stride=4)
            blk = pl.ds(r * quarter + c * BAND, BAND)
            m0, m1 = m_ref[0, sl, :], fm_ref[blk, :]
            mx = jnp.maximum(m0, m1)
            e0, e1 = jnp.exp(m0 - mx), jnp.exp(m1 - mx)
            den = e0 * l_ref[0, sl, :] + e1 * fl_ref[blk, :]
            num = e0 * n_ref[0, sl, :] + e1 * fn_ref[blk, :]
            o_ref[sl, :] = num / den


def _attn_prompt(q, k, v, slopes, batch, seq):
    assert [seq // d // BAND for _, d in DILATED_BRANCHES] == [16, 4, 1]
    nb = seq // BAND
    blk = pl.BlockSpec((seq, LANES), lambda b, p: (b, p))
    return pl.pallas_call(
        functools.partial(_attn_prompt_kernel, seq=seq),
        grid=(batch, WIDTH_A // LANES),
        in_specs=[pl.BlockSpec(memory_space=pltpu.SMEM), blk, blk, blk],
        out_specs=blk,
        out_shape=jax.ShapeDtypeStruct((batch * seq, WIDTH_A), F32),
        scratch_shapes=[pltpu.VMEM((seq, LANES), F32),
                        pltpu.VMEM((3, nb, 2, BAND, LANES), BF16),
                        pltpu.VMEM((3, BAND + seq, LANES), BF16),
                        pltpu.VMEM((3, BAND + seq, 2 * LANES), BF16),
                        pltpu.VMEM((5, 2 * BAND, 2 * BAND), F32),
                        pltpu.VMEM((3, 2 * BAND, 2 * BAND), F32),
                        pltpu.VMEM((3, 2 * BAND, 2 * BAND), BF16)]
        + [pltpu.VMEM((3, seq, LANES), F32)] * 3 + [pltpu.VMEM((seq, LANES), F32)] * 3,
        compiler_params=_params("parallel", "arbitrary"),
        name="attn_prompt",
    )(slopes, q, k, v)


def _attn_sample_kernel(q_ref, kn_ref, vn_ref, kt_ref, vt_ref, o_ref, *, t_new, w_buf, pad):
    rows = N_HEADS_A * t_new
    zeros = jnp.zeros((pad - t_new, WIDTH_A), F32)
    k_new = jnp.concatenate([kn_ref[...], zeros], axis=0).astype(BF16)
    v_new = jnp.concatenate([vn_ref[...], zeros], axis=0).astype(BF16)

    qrep = jnp.concatenate([q_ref[...]] * N_HEADS_A, axis=0)
    rr = lax.broadcasted_iota(jnp.int32, (rows, WIDTH_A), 0)
    ll = lax.broadcasted_iota(jnp.int32, (rows, WIDTH_A), 1)
    qrows = jnp.where(rr // t_new == ll // HEAD_DIM, qrep, 0.0).astype(BF16)
    s = jnp.concatenate([_mm(qrows, kt_ref[...].astype(BF16)), _mm_nt(qrows, k_new)], axis=1)

    r2 = lax.broadcasted_iota(jnp.int32, s.shape, 0)
    u2 = lax.broadcasted_iota(jnp.int32, s.shape, 1)
    d = w_buf + r2 % t_new - u2
    mult = jnp.zeros(s.shape, F32)
    for window, dil in DILATED_BRANCHES:
        mult = mult + jnp.where((d >= 0) & (d <= window) & (d % dil == 0), 1.0, 0.0)
    head = lax.broadcasted_iota(jnp.int32, (rows, 1), 0) // t_new
    slope = jnp.zeros((rows, 1), F32)
    for h in range(N_HEADS_A):
        slope = jnp.where(head == h, 2.0 ** -(h + 1), slope)
    s = jnp.where(mult > 0, s - d.astype(F32) * slope, NEG_BIG)
    m = jnp.max(s, axis=-1, keepdims=True)
    p = mult * jnp.exp(s - m)
    den = jnp.sum(p, axis=-1, keepdims=True)
    p = p.astype(BF16)
    o = (_mm_nt(p[:, 0:w_buf], vt_ref[...].astype(BF16)) + _mm(p[:, w_buf:w_buf + pad], v_new)) * (1.0 / den)

    lane_head = lax.broadcasted_iota(jnp.int32, (t_new, WIDTH_A), 1) // HEAD_DIM
    out = jnp.zeros((t_new, WIDTH_A), F32)
    for h in range(N_HEADS_A):
        out = jnp.where(lane_head == h, o[h * t_new:(h + 1) * t_new, :], out)
    o_ref[...] = out


def _attn_sample(q, k, v, cache_kt, cache_vt, t_new):
    batch, _, w_buf = cache_kt.shape
    assert t_new % SUBLANES == 0 and w_buf >= DILATED_BRANCHES[-1][0] and w_buf % LANES == 0
    pad = LANES
    new = pl.BlockSpec((t_new, WIDTH_A), lambda b: (b, 0))
    cache = pl.BlockSpec((None, WIDTH_A, w_buf), lambda b: (b, 0, 0))
    return pl.pallas_call(
        functools.partial(_attn_sample_kernel, t_new=t_new, w_buf=w_buf, pad=pad),
        grid=(batch,),
        in_specs=[new, new, new, cache, cache],
        out_specs=new,
        out_shape=jax.ShapeDtypeStruct((batch * t_new, WIDTH_A), F32),
        compiler_params=_params("parallel"),
        name="attn_sample",
    )(q, k, v, cache_kt, cache_vt)


def _mem_kv_kernel(mem_ref, g_ref, wk_ref, wv_ref, mk_ref, mv_ref, mkh_ref, mvh_ref):
    h = _rms(mem_ref[...], g_ref[...]).astype(BF16)
    mk = _mm(h, wk_ref[...])
    mv = _mm(h, wv_ref[...])
    mk_ref[...] = mk.astype(BF16)
    mv_ref[...] = mv.astype(BF16)
    mkh_ref[...] = mk.reshape(mkh_ref.shape)
    mvh_ref[...] = mv.reshape(mvh_ref.shape)


def _mem_kv(mem, g, wk, wv, tm):
    n = mem.shape[0]
    row = pl.BlockSpec((tm, D_MODEL), lambda i: (i, 0))
    heads = pl.BlockSpec((tm, N_HEADS_X, HEAD_DIM_X), lambda i: (i, 0, 0))
    return pl.pallas_call(
        _mem_kv_kernel,
        grid=(n // tm,),
        in_specs=[row, _const_spec((1, D_MODEL)), _const_spec(wk.shape), _const_spec(wv.shape)],
        out_specs=[row, row, heads, heads],
        out_shape=[jax.ShapeDtypeStruct((n, D_MODEL), BF16)] * 2
        + [jax.ShapeDtypeStruct((n, N_HEADS_X, HEAD_DIM_X), F32)] * 2,
        compiler_params=_params("parallel"),
        name="mem_kv",
    )(mem, g, wk, wv)


def _shifted_rows(first_row, rows):
    offs = [first_row + j for j in range(CONV_WIDTH) if (first_row + j) % SUBLANES]
    return max(offs) // SUBLANES * SUBLANES + rows


def _realign_conv_input(ext_ref, sh_ref):
    for s in range(1, SUBLANES):
        sh_ref[s - 1] = ext_ref[pl.ds(s, sh_ref.shape[1]), :]


def _conv_module(ext_ref, sh_ref, first_row, row0, rows, cw_ref, cb_ref, lg_ref, lb_ref):
    y = cb_ref[...]
    for j in range(CONV_WIDTH):
        base, s = (first_row + j) // SUBLANES * SUBLANES, (first_row + j) % SUBLANES
        at = pl.ds(base + row0, rows)
        y = y + cw_ref[j:j + 1, :] * (ext_ref[at, :] if s == 0 else sh_ref[s - 1, at, :])
    yc = y - jnp.mean(y, axis=-1, keepdims=True)
    yn = yc * lax.rsqrt(jnp.mean(yc * yc, axis=-1, keepdims=True) + EPS) * lg_ref[...] + lb_ref[...]
    return yn * jax.nn.sigmoid(yn)


def _softmax_rows(s):
    m = jnp.max(s, axis=-1, keepdims=True)
    p = jnp.exp(s - m)
    return p.astype(BF16), 1.0 / jnp.sum(p, axis=-1, keepdims=True)


def _mix_prompt_kernel(x_ref, u_ref, halo_ref, oa_ref, mk_ref, mv_ref, cw_ref, cb_ref, lg_ref, lb_ref,
                       wout_ref, gx_ref, wq_ref, wo_ref, y_ref, ext_ref, sh_ref):
    ts = x_ref.shape[0]
    halo = halo_ref[...]
    ext_ref[0:HALO, :] = jnp.where(pl.program_id(1) == 0, jnp.zeros_like(halo), halo)
    ext_ref[HALO:HALO + ts, :] = u_ref[...]
    _realign_conv_input(ext_ref, sh_ref)
    mk = mk_ref[...]
    mv = mv_ref[...]
    for r0 in range(0, ts, MIX_ROWS):
        rows = slice(r0, r0 + MIX_ROWS)
        ob = _conv_module(ext_ref, sh_ref, HALO - (CONV_WIDTH - 1), r0, MIX_ROWS, cw_ref, cb_ref, lg_ref, lb_ref)
        x2 = (x_ref[rows, :] + _mm(oa_ref[rows, :].astype(BF16), wout_ref[0:WIDTH_A, :])
              + _mm(ob.astype(BF16), wout_ref[WIDTH_A:WIDTH_A + WIDTH_B, :]))
        hx = _rms(x2, gx_ref[...]).astype(BF16)
        qx = (_mm(hx, wq_ref[...]) * (HEAD_DIM_X ** -0.5)).astype(BF16)
        outs = []
        for h in range(N_HEADS_X):
            sl = slice(h * HEAD_DIM_X, (h + 1) * HEAD_DIM_X)
            p, inv = _softmax_rows(_mm_nt(qx[:, sl], mk[:, sl]))
            outs.append((_mm(p, mv[:, sl]) * inv).astype(BF16))
        y_ref[rows, :] = x2 + _mm(jnp.concatenate(outs, axis=-1), wo_ref[...])


def _mix_prompt(x, u, oa, mk, mv, cw, cb, lg, lb, wout, gx, wq, wo, batch, seq, ts):
    tiles = seq // ts
    row = lambda w: pl.BlockSpec((ts, w), lambda b, i: (b * tiles + i, 0))
    halo = pl.BlockSpec((HALO, WIDTH_B),
                        lambda b, i: (jnp.maximum((b * seq + i * ts) // HALO - 1, 0), 0))
    mem = pl.BlockSpec((N_MEM, D_MODEL), lambda b, i: (b, 0))
    vec = lambda w: _const_spec((1, w))
    return pl.pallas_call(
        _mix_prompt_kernel,
        grid=(batch, tiles),
        in_specs=[row(D_MODEL), row(WIDTH_B), halo, row(WIDTH_A), mem, mem, _const_spec(cw.shape),
                  vec(WIDTH_B), vec(WIDTH_B), vec(WIDTH_B), _const_spec(wout.shape), vec(D_MODEL),
                  _const_spec(wq.shape), _const_spec(wo.shape)],
        out_specs=row(D_MODEL),
        out_shape=jax.ShapeDtypeStruct((batch * seq, D_MODEL), F32),
        scratch_shapes=[pltpu.VMEM((HALO + ts, WIDTH_B), F32),
                        pltpu.VMEM((SUBLANES - 1, _shifted_rows(HALO - (CONV_WIDTH - 1), ts), WIDTH_B), F32)],
        compiler_params=_params("parallel", "parallel"),
        name="mix_prompt",
    )(x, u, u, oa, mk, mv, cw, cb, lg, lb, wout, gx, wq, wo)


def _mix_sample_kernel(x_ref, uext_ref, oa_ref, mk_ref, mv_ref, cw_ref, cb_ref, lg_ref, lb_ref,
                       wout_ref, gx_ref, wq_ref, wo_ref, y_ref, sh_ref, *, group, t_new, t_ext):
    rows = group * t_ext
    _realign_conv_input(uext_ref, sh_ref)
    conv = _conv_module(uext_ref, sh_ref, 0, 0, rows, cw_ref, cb_ref, lg_ref, lb_ref)
    ob = conv.reshape(group, t_ext, WIDTH_B)[:, 0:t_new, :].reshape(group * t_new, WIDTH_B)
    x2 = (x_ref[...] + _mm(oa_ref[...].astype(BF16), wout_ref[0:WIDTH_A, :])
          + _mm(ob.astype(BF16), wout_ref[WIDTH_A:WIDTH_A + WIDTH_B, :]))

    hx = _rms(x2, gx_ref[...]).astype(BF16)
    qx = _mm(hx, wq_ref[...]) * (HEAD_DIM_X ** -0.5)
    qrows_n = N_HEADS_X * t_new
    rr = lax.broadcasted_iota(jnp.int32, (qrows_n, D_MODEL), 0)
    ll = lax.broadcasted_iota(jnp.int32, (qrows_n, D_MODEL), 1)
    own = rr // t_new == ll // HEAD_DIM_X
    lane_head = lax.broadcasted_iota(jnp.int32, (t_new, D_MODEL), 1) // HEAD_DIM_X
    outs = []
    for b in range(group):
        qb = qx[b * t_new:(b + 1) * t_new, :]
        qrows = jnp.where(own, jnp.concatenate([qb] * N_HEADS_X, axis=0), 0.0).astype(BF16)
        mk = mk_ref[b].reshape(N_MEM, D_MODEL).astype(BF16)
        mv = mv_ref[b].reshape(N_MEM, D_MODEL).astype(BF16)
        p, inv = _softmax_rows(_mm_nt(qrows, mk))
        o = _mm(p, mv) * inv
        out = jnp.zeros((t_new, D_MODEL), F32)
        for h in range(N_HEADS_X):
            out = jnp.where(lane_head == h, o[h * t_new:(h + 1) * t_new, :], out)
        outs.append(out)
    y_ref[...] = x2 + _mm(jnp.concatenate(outs, axis=0).astype(BF16), wo_ref[...])


def _mix_sample(x, uext, oa, mk, mv, cw, cb, lg, lb, wout, gx, wq, wo, batch, t_new, t_ext, group):
    rows = group * t_new
    row = lambda w: pl.BlockSpec((rows, w), lambda i: (i, 0))
    ext = pl.BlockSpec((group * t_ext + HALO, WIDTH_B), lambda i: (i, 0))
    mem = pl.BlockSpec((group, N_MEM, N_HEADS_X, HEAD_DIM_X), lambda i: (i, 0, 0, 0))
    vec = lambda w: _const_spec((1, w))
    return pl.pallas_call(
        functools.partial(_mix_sample_kernel, group=group, t_new=t_new, t_ext=t_ext),
        grid=(batch // group,),
        in_specs=[row(D_MODEL), ext, row(WIDTH_A), mem, mem, _const_spec(cw.shape),
                  vec(WIDTH_B), vec(WIDTH_B), vec(WIDTH_B), _const_spec(wout.shape), vec(D_MODEL),
                  _const_spec(wq.shape), _const_spec(wo.shape)],
        out_specs=row(D_MODEL),
        out_shape=jax.ShapeDtypeStruct((batch * t_new, D_MODEL), F32),
        scratch_shapes=[pltpu.VMEM((SUBLANES - 1, _shifted_rows(0, group * t_ext), WIDTH_B), F32)],
        compiler_params=_params("parallel"),
        name="mix_sample",
    )(x, uext, oa, mk, mv, cw, cb, lg, lb, wout, gx, wq, wo)


def kernel(x_prompt, x_sample, mem_prompt, cache_win_k, cache_win_v, cache_conv, cache_mem_k, cache_mem_v, ffn1_norm, ffn1_gate, ffn1_up, ffn1_down, mix_norm, w_in, conv_w, conv_b, conv_ln_g, conv_ln_b, w_out, xattn_norm, mem_norm, w_cq, w_ck, w_cv, w_co, ffn2_norm, ffn2_gate, ffn2_up, ffn2_down, final_norm):
    depth = ffn1_norm.shape[0]
    assert depth == 1
    bp, seq, _ = x_prompt.shape
    bs, t_new, _ = x_sample.shape
    keep = CONV_WIDTH - 1
    l = 0
    vec = lambda a: a.reshape(1, -1)
    bf = lambda a: a.astype(BF16)
    slopes = jnp.asarray([2.0 ** -(h + 1) for h in range(N_HEADS_A)], F32)

    f1 = (vec(ffn1_norm[l]), bf(ffn1_gate[l]), bf(ffn1_up[l]), bf(ffn1_down[l]))
    f2 = (vec(ffn2_norm[l]), bf(ffn2_gate[l]), bf(ffn2_up[l]), bf(ffn2_down[l]))
    gmix, win = vec(mix_norm[l]), bf(w_in[l])
    conv = (conv_w[l], vec(conv_b[l]), vec(conv_ln_g[l]), vec(conv_ln_b[l]))
    proj = (bf(w_out[l]), vec(xattn_norm[l]), bf(w_cq[l]), bf(w_co[l]))
    gfin = vec(final_norm)

    xp = x_prompt.reshape(bp * seq, D_MODEL)
    x1, q, k, v, u, kt, vt = _ffn_proj(xp, *f1, gmix, win, tm=512, seq=seq)
    oa = _attn_prompt(q, k, v, slopes, bp, seq)
    mk, mv, mkh, mvh = _mem_kv(mem_prompt.reshape(bp * N_MEM, D_MODEL), vec(mem_norm[l]), bf(w_ck[l]), bf(w_cv[l]),
                               tm=512)
    x3 = _mix_prompt(x1, u, oa, mk, mv, *conv, *proj, batch=bp, seq=seq, ts=512)
    yp = _ffn_final(x3, *f2, gfin, tm=512)

    xs = x_sample.reshape(bs * t_new, D_MODEL)
    s1, sq, sk, sv, su, skh, svh = _ffn_proj(xs, *f1, gmix, win, tm=bs * t_new)
    seq_minor = lambda c: jnp.transpose(c, (0, 2, 3, 1)).reshape(c.shape[0], WIDTH_A, c.shape[1])
    soa = _attn_sample(sq, sk, sv, seq_minor(cache_win_k[l]), seq_minor(cache_win_v[l]), t_new)
    u_ext = jnp.concatenate([cache_conv[l], su.reshape(bs, t_new, WIDTH_B)], axis=1)
    t_ext = -(-(keep + t_new) // 8) * 8
    group = 8
    u_pad = jnp.pad(u_ext, ((0, 0), (0, t_ext - keep - t_new), (0, 0))).reshape(bs // group, group * t_ext, WIDTH_B)
    u_pad = jnp.pad(u_pad, ((0, 0), (0, HALO), (0, 0))).reshape(-1, WIDTH_B)
    s3 = _mix_sample(s1, u_pad, soa, cache_mem_k[l], cache_mem_v[l], *conv, *proj,
                     batch=bs, t_new=t_new, t_ext=t_ext, group=group)
    ys = _ffn_final(s3, *f2, gfin, tm=bs * t_new)

    heads = lambda a, b, t: a.reshape(1, b, t, N_HEADS_A, HEAD_DIM)
    from_seq_minor = lambda a: jnp.transpose(a.reshape(1, bp, N_HEADS_A, HEAD_DIM, seq), (0, 1, 4, 2, 3))
    mem_heads = lambda a: a.reshape(1, bp, N_MEM, N_HEADS_X, HEAD_DIM_X)
    return (yp.reshape(bp, seq, D_MODEL),
            ys.reshape(bs, t_new, D_MODEL),
            from_seq_minor(kt), from_seq_minor(vt),
            u.reshape(1, bp, seq, WIDTH_B)[:, :, seq - keep:],
            mem_heads(mkh), mem_heads(mvh),
            heads(skh, bs, t_new), heads(svh, bs, t_new),
            u_ext[None, :, t_new:])
```

```python
import functools

import jax
import jax.numpy as jnp
from jax import lax
from jax.experimental import pallas as pl
from jax.experimental.pallas import tpu as pltpu

D_MODEL = 1024
HEAD_DIM = 64
N_HEADS_A = 8
WIDTH_A = N_HEADS_A * HEAD_DIM
WIDTH_B = D_MODEL - WIDTH_A
DILATED_BRANCHES = ((128, 1), (512, 4), (2048, 16))
BAND = 128
CONV_WIDTH = 31
D_FF = 2816
N_MEM = 256
N_HEADS_X = 4
HEAD_DIM_X = D_MODEL // N_HEADS_X
EPS = 1e-6

LANES = 128
SUBLANES = 8
FF_CHUNK = 256
HALO = 32
ATTN_UNROLL = 1
NEG_BIG = -1e30
VMEM_LIMIT = 56 * 1024 * 1024

F32 = jnp.float32
BF16 = jnp.bfloat16


def _const_spec(shape):
    nd = len(shape)
    return pl.BlockSpec(shape, lambda *_: (0,) * nd, pipeline_mode=pl.Buffered(1))


def _params(*sem):
    return pltpu.CompilerParams(dimension_semantics=sem, vmem_limit_bytes=VMEM_LIMIT)


def _rms(x, g):
    return x * lax.rsqrt(jnp.mean(x * x, axis=-1, keepdims=True) + EPS) * g


def _mm(a, b):
    return jnp.dot(a, b, preferred_element_type=F32)


def _mm_nt(a, b):
    return lax.dot_general(a, b, (((1,), (1,)), ((), ())), preferred_element_type=F32)


def _swiglu_half_step(x, g_ref, wg_ref, wu_ref, wd_ref, act_ref):
    h = _rms(x, g_ref[...]).astype(BF16)
    for c in range(0, D_FF, FF_CHUNK):
        gate = _mm(h, wg_ref[:, c:c + FF_CHUNK])
        up = _mm(h, wu_ref[:, c:c + FF_CHUNK])
        act_ref[:, c:c + FF_CHUNK] = (gate * jax.nn.sigmoid(gate) * up).astype(BF16)
    return x + 0.5 * _mm(act_ref[...], wd_ref[...])


def _ffn_proj_kernel(x_ref, g_ref, wg_ref, wu_ref, wd_ref, gmix_ref, win_ref,
                     x1_ref, q_ref, k_ref, v_ref, u_ref, kh_ref, vh_ref, act_ref):
    x1 = _swiglu_half_step(x_ref[...], g_ref, wg_ref, wu_ref, wd_ref, act_ref)
    x1_ref[...] = x1
    h = _rms(x1, gmix_ref[...]).astype(BF16)
    w = WIDTH_A
    q_ref[...] = _mm(h, win_ref[:, 0:w]) * (HEAD_DIM ** -0.5)
    k = _mm(h, win_ref[:, w:2 * w])
    v = _mm(h, win_ref[:, 2 * w:3 * w])
    k_ref[...] = k
    v_ref[...] = v
    if len(kh_ref.shape) == 2:
        kh_ref[...] = k.T
        vh_ref[...] = v.T
    else:
        kh_ref[...] = k.reshape(kh_ref.shape)
        vh_ref[...] = v.reshape(vh_ref.shape)
    a = _mm(h, win_ref[:, 3 * w:3 * w + WIDTH_B])
    g = _mm(h, win_ref[:, 3 * w + WIDTH_B:3 * w + 2 * WIDTH_B])
    u_ref[...] = a * jax.nn.sigmoid(g)


def _ffn_proj(x, g, wg, wu, wd, gmix, win, tm, seq=None):
    n = x.shape[0]
    row = lambda w: pl.BlockSpec((tm, w), lambda i: (i, 0))
    if seq is None:
        heads = pl.BlockSpec((tm, N_HEADS_A, HEAD_DIM), lambda i: (i, 0, 0))
        heads_shape = jax.ShapeDtypeStruct((n, N_HEADS_A, HEAD_DIM), F32)
    else:
        tiles = seq // tm
        heads = pl.BlockSpec((None, WIDTH_A, tm), lambda i: (i // tiles, 0, i % tiles))
        heads_shape = jax.ShapeDtypeStruct((n // seq, WIDTH_A, seq), F32)
    return pl.pallas_call(
        _ffn_proj_kernel,
        grid=(n // tm,),
        in_specs=[row(D_MODEL), _const_spec((1, D_MODEL)), _const_spec(wg.shape), _const_spec(wu.shape),
                  _const_spec(wd.shape), _const_spec((1, D_MODEL)), _const_spec(win.shape)],
        out_specs=[row(D_MODEL), row(WIDTH_A), row(WIDTH_A), row(WIDTH_A), row(WIDTH_B), heads, heads],
        out_shape=[jax.ShapeDtypeStruct((n, D_MODEL), F32)] + [jax.ShapeDtypeStruct((n, WIDTH_A), F32)] * 4
        + [heads_shape] * 2,
        scratch_shapes=[pltpu.VMEM((tm, D_FF), BF16)],
        compiler_params=_params("parallel"),
        name="ffn_proj",
    )(x, g, wg, wu, wd, gmix, win)


def _ffn_final_kernel(x_ref, g_ref, wg_ref, wu_ref, wd_ref, gfin_ref, y_ref, act_ref):
    x1 = _swiglu_half_step(x_ref[...], g_ref, wg_ref, wu_ref, wd_ref, act_ref)
    y_ref[...] = _rms(x1, gfin_ref[...])


def _ffn_final(x, g, wg, wu, wd, gfin, tm):
    n = x.shape[0]
    row = pl.BlockSpec((tm, D_MODEL), lambda i: (i, 0))
    return pl.pallas_call(
        _ffn_final_kernel,
        grid=(n // tm,),
        in_specs=[row, _const_spec((1, D_MODEL)), _const_spec(wg.shape), _const_spec(wu.shape),
                  _const_spec(wd.shape), _const_spec((1, D_MODEL))],
        out_specs=row,
        out_shape=jax.ShapeDtypeStruct((n, D_MODEL), F32),
        scratch_shapes=[pltpu.VMEM((tm, D_FF), BF16)],
        compiler_params=_params("parallel"),
        name="ffn_final",
    )(x, g, wg, wu, wd, gfin)


def _attn_prompt_kernel(slopes_ref, q_ref, k_ref, v_ref, o_ref,
                        x4_ref, qs_ref, kp_ref, vp_ref, bias_ref, s_ref, p_ref, m_ref, l_ref, n_ref,
                        fm_ref, fl_ref, fn_ref, *, seq):
    pair = pl.program_id(0)
    nb = seq // BAND
    lo = lax.broadcasted_iota(jnp.int32, (BAND, LANES), 1) < HEAD_DIM

    @pl.when(pl.program_id(1) == 0)
    def _():
        kp_ref[:, 0:BAND, :] = jnp.zeros((3, BAND, LANES), BF16)
        vp_ref[:, 0:BAND, 0:LANES] = jnp.zeros((3, BAND, LANES), BF16)
        vp_ref[:, :, LANES:2 * LANES] = jnp.ones((3, BAND + seq, LANES), BF16)
        p_ref[...] = jnp.zeros(p_ref.shape, BF16)
        qi = lax.broadcasted_iota(jnp.int32, (BAND, 2 * BAND), 0)
        kj = lax.broadcasted_iota(jnp.int32, (BAND, 2 * BAND), 1)
        dist = qi + BAND - kj
        neg_dist = jnp.where((dist >= 0) & (dist <= BAND), -dist.astype(F32), NEG_BIG)
        neg_dist_cur = jnp.where(kj >= BAND, neg_dist, NEG_BIG)
        for b, (_, dil) in enumerate(DILATED_BRANCHES):
            for noprev in range(2):
                if 2 * b + noprev < bias_ref.shape[0]:
                    table = neg_dist_cur if noprev else neg_dist
                    for half in range(2):
                        bias_ref[2 * b + noprev, half * BAND:(half + 1) * BAND, :] = (
                            (table * slopes_ref[2 * pair + half]) * float(dil))

    def put(kind, b, first, x):
        nblocks = x.shape[0] // BAND
        if kind == 0:
            is_lo = lax.broadcasted_iota(jnp.int32, x.shape, 1) < HEAD_DIM
            x_lo = jnp.where(is_lo, x, 0.0).astype(BF16)
            x_hi = jnp.where(is_lo, 0.0, x).astype(BF16)
            for j in range(nblocks):
                qs_ref[b, first + j, 0] = x_lo[j * BAND:(j + 1) * BAND]
                qs_ref[b, first + j, 1] = x_hi[j * BAND:(j + 1) * BAND]
        elif kind == 1:
            kp_ref[b, BAND + first * BAND:BAND + first * BAND + x.shape[0], :] = x.astype(BF16)
        else:
            vp_ref[b, BAND + first * BAND:BAND + first * BAND + x.shape[0], 0:LANES] = x.astype(BF16)

    quarter = seq // 4
    for kind, ref in enumerate((q_ref, k_ref, v_ref)):
        for r in range(4):
            put(kind, 0, 4 * r, ref[r * quarter:(r + 1) * quarter, :])
            x = ref[pl.ds(r, quarter, stride=4), :]
            x4_ref[r * quarter:(r + 1) * quarter, :] = x
            put(kind, 1, 4 * r, x)
        for r in range(16):
            put(kind, 2, r, x4_ref[pl.ds((r % 4) * quarter + r // 4, seq // 16, stride=4), :])

    def blocks_per_class(b):
        return seq // DILATED_BRANCHES[b][1] // BAND

    def scores(n):
        out = []
        for b in range(3):
            qb = qs_ref[b, n].reshape(2 * BAND, LANES)
            if blocks_per_class(b) > 1:
                kb = kp_ref[b, pl.ds(pl.multiple_of(n * BAND, BAND), 2 * BAND), :]
                bias = bias_ref[2 * b + (n % blocks_per_class(b) == 0).astype(jnp.int32)]
            else:
                kb = kp_ref[b, pl.ds(pl.multiple_of(n * BAND + BAND, BAND), BAND), :]
                bias = bias_ref[2 * b, :, BAND:2 * BAND]
            out.append(_mm_nt(qb, kb) + bias)
        return out

    def keys_of(b):
        return 2 * BAND if blocks_per_class(b) > 1 else BAND

    def softmax(n):
        rows = pl.ds(pl.multiple_of(n * BAND, BAND), BAND)
        for b in range(3):
            s = s_ref[b, :, 0:keys_of(b)]
            m = jnp.max(s, axis=-1, keepdims=True)
            p_ref[b, :, 0:keys_of(b)] = jnp.exp(s - m).astype(BF16)
            m_ref[b, rows, :] = jnp.where(lo, m[0:BAND], m[BAND:2 * BAND])

    def weighted_values(n):
        out = []
        for b in range(3):
            first = n * BAND + (2 * BAND - keys_of(b))
            vb = vp_ref[b, pl.ds(pl.multiple_of(first, BAND), keys_of(b)), :]
            out.append(_mm(p_ref[b, :, 0:keys_of(b)], vb))
        return out

    def step(n):
        s_next = scores(jnp.minimum(n + 1, nb - 1))
        done = jnp.clip(n - 1, 0, nb - 1)
        pv = weighted_values(done)
        softmax(jnp.minimum(n, nb - 1))
        rows = pl.ds(pl.multiple_of(done * BAND, BAND), BAND)
        for b in range(3):
            s_ref[b, :, 0:keys_of(b)] = s_next[b]
            n_ref[b, rows, :] = jnp.where(lo, pv[b][0:BAND, 0:LANES], pv[b][BAND:2 * BAND, 0:LANES])
            l_ref[b, rows, :] = jnp.where(lo, pv[b][0:BAND, LANES:2 * LANES], pv[b][BAND:2 * BAND, LANES:2 * LANES])

    def trip(i, carry):
        for j in range(ATTN_UNROLL):
            step(i * ATTN_UNROLL + j)
        return carry

    for b, s0 in enumerate(scores(jnp.int32(0))):
        s_ref[b, :, 0:keys_of(b)] = s0
    lax.fori_loop(0, -(-(nb + 1) // ATTN_UNROLL), trip, 0)

    per16 = seq // 16
    for r in range(16):
        sl4 = pl.ds((r % 4) * quarter + r // 4, per16, stride=4)
        blk = pl.ds(r * per16, per16)
        m1, m2 = m_ref[1, sl4, :], m_ref[2, blk, :]
        mx = jnp.maximum(m1, m2)
        e1, e2 = jnp.exp(m1 - mx), jnp.exp(m2 - mx)
        fm_ref[sl4, :] = mx
        fl_ref[sl4, :] = e1 * l_ref[1, sl4, :] + e2 * l_ref[2, blk, :]
        fn_ref[sl4, :] = e1 * n_ref[1, sl4, :] + e2 * n_ref[2, blk, :]
    for r in range(4):
        for c in range(quarter // BAND):
            sl = pl.ds(r + 4 * BAND * c, BAND, stride=4)
            blk = pl.ds(r * quarter + c * BAND, BAND)
            m0, m1 = m_ref[0, sl, :], fm_ref[blk, :]
            mx = jnp.maximum(m0, m1)
            e0, e1 = jnp.exp(m0 - mx), jnp.exp(m1 - mx)
            den = e0 * l_ref[0, sl, :] + e1 * fl_ref[blk, :]
            num = e0 * n_ref[0, sl, :] + e1 * fn_ref[blk, :]
            o_ref[sl, :] = num / den


def _attn_prompt(q, k, v, slopes, batch, seq):
    assert [seq // d // BAND for _, d in DILATED_BRANCHES] == [16, 4, 1]
    nb = seq // BAND
    blk = pl.BlockSpec((seq, LANES), lambda p, b: (b, p))
    return pl.pallas_call(
        functools.partial(_attn_prompt_kernel, seq=seq),
        grid=(WIDTH_A // LANES, batch),
        in_specs=[pl.BlockSpec(memory_space=pltpu.SMEM), blk, blk, blk],
        out_specs=blk,
        out_shape=jax.ShapeDtypeStruct((batch * seq, WIDTH_A), F32),
        scratch_shapes=[pltpu.VMEM((seq, LANES), F32),
                        pltpu.VMEM((3, nb, 2, BAND, LANES), BF16),
                        pltpu.VMEM((3, BAND + seq, LANES), BF16),
                        pltpu.VMEM((3, BAND + seq, 2 * LANES), BF16),
                        pltpu.VMEM((5, 2 * BAND, 2 * BAND), F32),
                        pltpu.VMEM((3, 2 * BAND, 2 * BAND), F32),
                        pltpu.VMEM((3, 2 * BAND, 2 * BAND), BF16)]
        + [pltpu.VMEM((3, seq, LANES), F32)] * 3 + [pltpu.VMEM((seq, LANES), F32)] * 3,
        compiler_params=_params("parallel", "arbitrary"),
        name="attn_prompt",
    )(slopes, q, k, v)


def _attn_sample_kernel(q_ref, kn_ref, vn_ref, kt_ref, vt_ref, o_ref, *, t_new, w_buf, pad):
    rows = N_HEADS_A * t_new
    zeros = jnp.zeros((pad - t_new, WIDTH_A), F32)
    k_new = jnp.concatenate([kn_ref[...], zeros], axis=0).astype(BF16)
    v_new = jnp.concatenate([vn_ref[...], zeros], axis=0).astype(BF16)

    qrep = jnp.concatenate([q_ref[...]] * N_HEADS_A, axis=0)
    rr = lax.broadcasted_iota(jnp.int32, (rows, WIDTH_A), 0)
    ll = lax.broadcasted_iota(jnp.int32, (rows, WIDTH_A), 1)
    qrows = jnp.where(rr // t_new == ll // HEAD_DIM, qrep, 0.0).astype(BF16)
    s = jnp.concatenate([_mm(qrows, kt_ref[...].astype(BF16)), _mm_nt(qrows, k_new)], axis=1)

    r2 = lax.broadcasted_iota(jnp.int32, s.shape, 0)
    u2 = lax.broadcasted_iota(jnp.int32, s.shape, 1)
    d = w_buf + r2 % t_new - u2
    mult = jnp.zeros(s.shape, F32)
    for window, dil in DILATED_BRANCHES:
        mult = mult + jnp.where((d >= 0) & (d <= window) & (d % dil == 0), 1.0, 0.0)
    head = lax.broadcasted_iota(jnp.int32, (rows, 1), 0) // t_new
    slope = jnp.zeros((rows, 1), F32)
    for h in range(N_HEADS_A):
        slope = jnp.where(head == h, 2.0 ** -(h + 1), slope)
    s = jnp.where(mult > 0, s - d.astype(F32) * slope, NEG_BIG)
    m = jnp.max(s, axis=-1, keepdims=True)
    p = mult * jnp.exp(s - m)
    den = jnp.sum(p, axis=-1, keepdims=True)
    p = p.astype(BF16)
    o = (_mm_nt(p[:, 0:w_buf], vt_ref[...].astype(BF16)) + _mm(p[:, w_buf:w_buf + pad], v_new)) * (1.0 / den)

    lane_head = lax.broadcasted_iota(jnp.int32, (t_new, WIDTH_A), 1) // HEAD_DIM
    out = jnp.zeros((t_new, WIDTH_A), F32)
    for h in range(N_HEADS_A):
        out = jnp.where(lane_head == h, o[h * t_new:(h + 1) * t_new, :], out)
    o_ref[...] = out


def _attn_sample(q, k, v, cache_kt, cache_vt, t_new):
    batch, _, w_buf = cache_kt.shape
    assert t_new % SUBLANES == 0 and w_buf >= DILATED_BRANCHES[-1][0] and w_buf % LANES == 0
    pad = LANES
    new = pl.BlockSpec((t_new, WIDTH_A), lambda b: (b, 0))
    cache = pl.BlockSpec((None, WIDTH_A, w_buf), lambda b: (b, 0, 0))
    return pl.pallas_call(
        functools.partial(_attn_sample_kernel, t_new=t_new, w_buf=w_buf, pad=pad),
        grid=(batch,),
        in_specs=[new, new, new, cache, cache],
        out_specs=new,
        out_shape=jax.ShapeDtypeStruct((batch * t_new, WIDTH_A), F32),
        compiler_params=_params("parallel"),
        name="attn_sample",
    )(q, k, v, cache_kt, cache_vt)


def _mem_kv_kernel(mem_ref, g_ref, wk_ref, wv_ref, mk_ref, mv_ref, mkh_ref, mvh_ref):
    h = _rms(mem_ref[...], g_ref[...]).astype(BF16)
    mk = _mm(h, wk_ref[...])
    mv = _mm(h, wv_ref[...])
    mk_ref[...] = mk.astype(BF16)
    mv_ref[...] = mv.astype(BF16)
    mkh_ref[...] = mk.reshape(mkh_ref.shape)
    mvh_ref[...] = mv.reshape(mvh_ref.shape)


def _mem_kv(mem, g, wk, wv, tm):
    n = mem.shape[0]
    row = pl.BlockSpec((tm, D_MODEL), lambda i: (i, 0))
    heads = pl.BlockSpec((tm, N_HEADS_X, HEAD_DIM_X), lambda i: (i, 0, 0))
    return pl.pallas_call(
        _mem_kv_kernel,
        grid=(n // tm,),
        in_specs=[row, _const_spec((1, D_MODEL)), _const_spec(wk.shape), _const_spec(wv.shape)],
        out_specs=[row, row, heads, heads],
        out_shape=[jax.ShapeDtypeStruct((n, D_MODEL), BF16)] * 2
        + [jax.ShapeDtypeStruct((n, N_HEADS_X, HEAD_DIM_X), F32)] * 2,
        compiler_params=_params("parallel"),
        name="mem_kv",
    )(mem, g, wk, wv)


def _shifted_rows(first_row, rows):
    offs = [first_row + j for j in range(CONV_WIDTH) if (first_row + j) % SUBLANES]
    return max(offs) // SUBLANES * SUBLANES + rows


def _realign_conv_input(ext_ref, sh_ref):
    for s in range(1, SUBLANES):
        sh_ref[s - 1] = ext_ref[pl.ds(s, sh_ref.shape[1]), :]


def _conv_module(ext_ref, sh_ref, first_row, row0, rows, cw_ref, cb_ref, lg_ref, lb_ref):
    y = cb_ref[...]
    for j in range(CONV_WIDTH):
        base, s = (first_row + j) // SUBLANES * SUBLANES, (first_row + j) % SUBLANES
        at = pl.ds(base + row0, rows)
        y = y + cw_ref[j:j + 1, :] * (ext_ref[at, :] if s == 0 else sh_ref[s - 1, at, :])
    yc = y - jnp.mean(y, axis=-1, keepdims=True)
    yn = yc * lax.rsqrt(jnp.mean(yc * yc, axis=-1, keepdims=True) + EPS) * lg_ref[...] + lb_ref[...]
    return yn * jax.nn.sigmoid(yn)


def _softmax_rows(s):
    m = jnp.max(s, axis=-1, keepdims=True)
    p = jnp.exp(s - m)
    return p.astype(BF16), 1.0 / jnp.sum(p, axis=-1, keepdims=True)


def _mix_prompt_kernel(x_ref, u_ref, halo_ref, oa_ref, mk_ref, mv_ref, cw_ref, cb_ref, lg_ref, lb_ref,
                       wout_ref, gx_ref, wq_ref, wo_ref, y_ref, ext_ref, sh_ref):
    ts = x_ref.shape[0]
    halo = halo_ref[...]
    ext_ref[0:HALO, :] = jnp.where(pl.program_id(1) == 0, jnp.zeros_like(halo), halo)
    ext_ref[HALO:HALO + ts, :] = u_ref[...]
    _realign_conv_input(ext_ref, sh_ref)
    ob = _conv_module(ext_ref, sh_ref, HALO - (CONV_WIDTH - 1), 0, ts, cw_ref, cb_ref, lg_ref, lb_ref)
    x2 = (x_ref[...] + _mm(oa_ref[...].astype(BF16), wout_ref[0:WIDTH_A, :])
          + _mm(ob.astype(BF16), wout_ref[WIDTH_A:WIDTH_A + WIDTH_B, :]))
    hx = _rms(x2, gx_ref[...]).astype(BF16)
    qx = (_mm(hx, wq_ref[...]) * (HEAD_DIM_X ** -0.5)).astype(BF16)
    outs = []
    for h in range(N_HEADS_X):
        sl = slice(h * HEAD_DIM_X, (h + 1) * HEAD_DIM_X)
        p, inv = _softmax_rows(_mm_nt(qx[:, sl], mk_ref[:, sl]))
        outs.append((_mm(p, mv_ref[:, sl]) * inv).astype(BF16))
    y_ref[...] = x2 + _mm(jnp.concatenate(outs, axis=-1), wo_ref[...])


def _mix_prompt(x, u, oa, mk, mv, cw, cb, lg, lb, wout, gx, wq, wo, batch, seq, ts):
    tiles = seq // ts
    row = lambda w: pl.BlockSpec((ts, w), lambda b, i: (b * tiles + i, 0))
    halo = pl.BlockSpec((HALO, WIDTH_B),
                        lambda b, i: (jnp.maximum((b * seq + i * ts) // HALO - 1, 0), 0))
    mem = pl.BlockSpec((N_MEM, D_MODEL), lambda b, i: (b, 0))
    vec = lambda w: _const_spec((1, w))
    return pl.pallas_call(
        _mix_prompt_kernel,
        grid=(batch, tiles),
        in_specs=[row(D_MODEL), row(WIDTH_B), halo, row(WIDTH_A), mem, mem, _const_spec(cw.shape),
                  vec(WIDTH_B), vec(WIDTH_B), vec(WIDTH_B), _const_spec(wout.shape), vec(D_MODEL),
                  _const_spec(wq.shape), _const_spec(wo.shape)],
        out_specs=row(D_MODEL),
        out_shape=jax.ShapeDtypeStruct((batch * seq, D_MODEL), F32),
        scratch_shapes=[pltpu.VMEM((HALO + ts, WIDTH_B), F32),
                        pltpu.VMEM((SUBLANES - 1, _shifted_rows(HALO - (CONV_WIDTH - 1), ts), WIDTH_B), F32)],
        compiler_params=_params("parallel", "parallel"),
        name="mix_prompt",
    )(x, u, u, oa, mk, mv, cw, cb, lg, lb, wout, gx, wq, wo)


def _mix_sample_kernel(x_ref, uext_ref, oa_ref, mk_ref, mv_ref, cw_ref, cb_ref, lg_ref, lb_ref,
                       wout_ref, gx_ref, wq_ref, wo_ref, y_ref, sh_ref, *, group, t_new, t_ext):
    rows = group * t_ext
    _realign_conv_input(uext_ref, sh_ref)
    conv = _conv_module(uext_ref, sh_ref, 0, 0, rows, cw_ref, cb_ref, lg_ref, lb_ref)
    ob = conv.reshape(group, t_ext, WIDTH_B)[:, 0:t_new, :].reshape(group * t_new, WIDTH_B)
    x2 = (x_ref[...] + _mm(oa_ref[...].astype(BF16), wout_ref[0:WIDTH_A, :])
          + _mm(ob.astype(BF16), wout_ref[WIDTH_A:WIDTH_A + WIDTH_B, :]))

    hx = _rms(x2, gx_ref[...]).astype(BF16)
    qx = _mm(hx, wq_ref[...]) * (HEAD_DIM_X ** -0.5)
    qrows_n = N_HEADS_X * t_new
    rr = lax.broadcasted_iota(jnp.int32, (qrows_n, D_MODEL), 0)
    ll = lax.broadcasted_iota(jnp.int32, (qrows_n, D_MODEL), 1)
    own = rr // t_new == ll // HEAD_DIM_X
    lane_head = lax.broadcasted_iota(jnp.int32, (t_new, D_MODEL), 1) // HEAD_DIM_X
    outs = []
    for b in range(group):
        qb = qx[b * t_new:(b + 1) * t_new, :]
        qrows = jnp.where(own, jnp.concatenate([qb] * N_HEADS_X, axis=0), 0.0).astype(BF16)
        mk = mk_ref[b].reshape(N_MEM, D_MODEL).astype(BF16)
        mv = mv_ref[b].reshape(N_MEM, D_MODEL).astype(BF16)
        p, inv = _softmax_rows(_mm_nt(qrows, mk))
        o = _mm(p, mv) * inv
        out = jnp.zeros((t_new, D_MODEL), F32)
        for h in range(N_HEADS_X):
            out = jnp.where(lane_head == h, o[h * t_new:(h + 1) * t_new, :], out)
        outs.append(out)
    y_ref[...] = x2 + _mm(jnp.concatenate(outs, axis=0).astype(BF16), wo_ref[...])


def _mix_sample(x, uext, oa, mk, mv, cw, cb, lg, lb, wout, gx, wq, wo, batch, t_new, t_ext, group):
    rows = group * t_new
    row = lambda w: pl.BlockSpec((rows, w), lambda i: (i, 0))
    ext = pl.BlockSpec((group * t_ext + HALO, WIDTH_B), lambda i: (i, 0))
    mem = pl.BlockSpec((group, N_MEM, N_HEADS_X, HEAD_DIM_X), lambda i: (i, 0, 0, 0))
    vec = lambda w: _const_spec((1, w))
    return pl.pallas_call(
        functools.partial(_mix_sample_kernel, group=group, t_new=t_new, t_ext=t_ext),
        grid=(batch // group,),
        in_specs=[row(D_MODEL), ext, row(WIDTH_A), mem, mem, _const_spec(cw.shape),
                  vec(WIDTH_B), vec(WIDTH_B), vec(WIDTH_B), _const_spec(wout.shape), vec(D_MODEL),
                  _const_spec(wq.shape), _const_spec(wo.shape)],
        out_specs=row(D_MODEL),
        out_shape=jax.ShapeDtypeStruct((batch * t_new, D_MODEL), F32),
        scratch_shapes=[pltpu.VMEM((SUBLANES - 1, _shifted_rows(0, group * t_ext), WIDTH_B), F32)],
        compiler_params=_params("parallel"),
        name="mix_sample",
    )(x, uext, oa, mk, mv, cw, cb, lg, lb, wout, gx, wq, wo)


def kernel(x_prompt, x_sample, mem_prompt, cache_win_k, cache_win_v, cache_conv, cache_mem_k, cache_mem_v, ffn1_norm, ffn1_gate, ffn1_up, ffn1_down, mix_norm, w_in, conv_w, conv_b, conv_ln_g, conv_ln_b, w_out, xattn_norm, mem_norm, w_cq, w_ck, w_cv, w_co, ffn2_norm, ffn2_gate, ffn2_up, ffn2_down, final_norm):
    depth = ffn1_norm.shape[0]
    assert depth == 1
    bp, seq, _ = x_prompt.shape
    bs, t_new, _ = x_sample.shape
    keep = CONV_WIDTH - 1
    l = 0
    vec = lambda a: a.reshape(1, -1)
    bf = lambda a: a.astype(BF16)
    slopes = jnp.asarray([2.0 ** -(h + 1) for h in range(N_HEADS_A)], F32)

    f1 = (vec(ffn1_norm[l]), bf(ffn1_gate[l]), bf(ffn1_up[l]), bf(ffn1_down[l]))
    f2 = (vec(ffn2_norm[l]), bf(ffn2_gate[l]), bf(ffn2_up[l]), bf(ffn2_down[l]))
    gmix, win = vec(mix_norm[l]), bf(w_in[l])
    conv = (conv_w[l], vec(conv_b[l]), vec(conv_ln_g[l]), vec(conv_ln_b[l]))
    proj = (bf(w_out[l]), vec(xattn_norm[l]), bf(w_cq[l]), bf(w_co[l]))
    gfin = vec(final_norm)

    xp = x_prompt.reshape(bp * seq, D_MODEL)
    x1, q, k, v, u, kt, vt = _ffn_proj(xp, *f1, gmix, win, tm=512, seq=seq)
    oa = _attn_prompt(q, k, v, slopes, bp, seq)
    mk, mv, mkh, mvh = _mem_kv(mem_prompt.reshape(bp * N_MEM, D_MODEL), vec(mem_norm[l]), bf(w_ck[l]), bf(w_cv[l]),
                               tm=512)
    x3 = _mix_prompt(x1, u, oa, mk, mv, *conv, *proj, batch=bp, seq=seq, ts=512)
    yp = _ffn_final(x3, *f2, gfin, tm=1024)

    xs = x_sample.reshape(bs * t_new, D_MODEL)
    s1, sq, sk, sv, su, skh, svh = _ffn_proj(xs, *f1, gmix, win, tm=bs * t_new)
    seq_minor = lambda c: jnp.transpose(c, (0, 2, 3, 1)).reshape(c.shape[0], WIDTH_A, c.shape[1])
    soa = _attn_sample(sq, sk, sv, seq_minor(cache_win_k[l]), seq_minor(cache_win_v[l]), t_new)
    u_ext = jnp.concatenate([cache_conv[l], su.reshape(bs, t_new, WIDTH_B)], axis=1)
    t_ext = -(-(keep + t_new) // 8) * 8
    group = 8
    u_pad = jnp.pad(u_ext, ((0, 0), (0, t_ext - keep - t_new), (0, 0))).reshape(bs // group, group * t_ext, WIDTH_B)
    u_pad = jnp.pad(u_pad, ((0, 0), (0, HALO), (0, 0))).reshape(-1, WIDTH_B)
    s3 = _mix_sample(s1, u_pad, soa, cache_mem_k[l], cache_mem_v[l], *conv, *proj,
                     batch=bs, t_new=t_new, t_ext=t_ext, group=group)
    ys = _ffn_final(s3, *f2, gfin, tm=bs * t_new)

    heads = lambda a, b, t: a.reshape(1, b, t, N_HEADS_A, HEAD_DIM)
    from_seq_minor = lambda a: jnp.transpose(a.reshape(1, bp, N_HEADS_A, HEAD_DIM, seq), (0, 1, 4, 2, 3))
    mem_heads = lambda a: a.reshape(1, bp, N_MEM, N_HEADS_X, HEAD_DIM_X)
    return (yp.reshape(bp, seq, D_MODEL),
            ys.reshape(bs, t_new, D_MODEL),
            from_seq_minor(kt), from_seq_minor(vt),
            u.reshape(1, bp, seq, WIDTH_B)[:, :, seq - keep:],
            mem_heads(mkh), mem_heads(mvh),
            heads(skh, bs, t_new), heads(svh, bs, t_new),
            u_ext[None, :, t_new:])
```

```python
import functools

import jax
import jax.numpy as jnp
from jax import lax
from jax.experimental import pallas as pl
from jax.experimental.pallas import tpu as pltpu

D_MODEL = 1024
HEAD_DIM = 64
N_HEADS_A = 8
WIDTH_A = N_HEADS_A * HEAD_DIM
WIDTH_B = D_MODEL - WIDTH_A
DILATED_BRANCHES = ((128, 1), (512, 4), (2048, 16))
BAND = 128
CONV_WIDTH = 31
D_FF = 2816
N_MEM = 256
N_HEADS_X = 4
HEAD_DIM_X = D_MODEL // N_HEADS_X
EPS = 1e-6

LANES = 128
SUBLANES = 8
FF_CHUNK = 256
HALO = 32
ATTN_UNROLL = 1
NEG_BIG = -1e30
VMEM_LIMIT = 56 * 1024 * 1024

F32 = jnp.float32
BF16 = jnp.bfloat16


def _const_spec(shape):
    nd = len(shape)
    return pl.BlockSpec(shape, lambda *_: (0,) * nd, pipeline_mode=pl.Buffered(1))


def _params(*sem):
    return pltpu.CompilerParams(dimension_semantics=sem, vmem_limit_bytes=VMEM_LIMIT)


def _rms(x, g):
    return x * lax.rsqrt(jnp.mean(x * x, axis=-1, keepdims=True) + EPS) * g


def _mm(a, b):
    return jnp.dot(a, b, preferred_element_type=F32)


def _mm_nt(a, b):
    return lax.dot_general(a, b, (((1,), (1,)), ((), ())), preferred_element_type=F32)


def _swiglu_half_step(x, g_ref, wg_ref, wu_ref, wd_ref, act_ref):
    h = _rms(x, g_ref[...]).astype(BF16)
    for c in range(0, D_FF, FF_CHUNK):
        gate = _mm(h, wg_ref[:, c:c + FF_CHUNK])
        up = _mm(h, wu_ref[:, c:c + FF_CHUNK])
        act_ref[:, c:c + FF_CHUNK] = (gate * jax.nn.sigmoid(gate) * up).astype(BF16)
    return x + 0.5 * _mm(act_ref[...], wd_ref[...])


def _ffn_proj_kernel(x_ref, g_ref, wg_ref, wu_ref, wd_ref, gmix_ref, win_ref,
                     x1_ref, q_ref, k_ref, v_ref, u_ref, kh_ref, vh_ref, act_ref):
    def put_cols(ref, val):
        if len(ref.shape) == 2:
            ref[...] = val
        else:
            for p in range(ref.shape[0]):
                ref[p] = val[:, p * LANES:(p + 1) * LANES]

    x1 = _swiglu_half_step(x_ref[...], g_ref, wg_ref, wu_ref, wd_ref, act_ref)
    x1_ref[...] = x1
    h = _rms(x1, gmix_ref[...]).astype(BF16)
    w = WIDTH_A
    put_cols(q_ref, _mm(h, win_ref[:, 0:w]) * (HEAD_DIM ** -0.5))
    k = _mm(h, win_ref[:, w:2 * w])
    v = _mm(h, win_ref[:, 2 * w:3 * w])
    put_cols(k_ref, k)
    put_cols(v_ref, v)
    if len(kh_ref.shape) == 2:
        kh_ref[...] = k.T
        vh_ref[...] = v.T
    else:
        kh_ref[...] = k.reshape(kh_ref.shape)
        vh_ref[...] = v.reshape(vh_ref.shape)
    a = _mm(h, win_ref[:, 3 * w:3 * w + WIDTH_B])
    g = _mm(h, win_ref[:, 3 * w + WIDTH_B:3 * w + 2 * WIDTH_B])
    u_ref[...] = a * jax.nn.sigmoid(g)


def _ffn_proj(x, g, wg, wu, wd, gmix, win, tm, seq=None):
    n = x.shape[0]
    row = lambda w: pl.BlockSpec((tm, w), lambda i: (i, 0))
    if seq is None:
        heads = pl.BlockSpec((tm, N_HEADS_A, HEAD_DIM), lambda i: (i, 0, 0))
        heads_shape = jax.ShapeDtypeStruct((n, N_HEADS_A, HEAD_DIM), F32)
        qkv = row(WIDTH_A)
        qkv_shape = jax.ShapeDtypeStruct((n, WIDTH_A), F32)
    else:
        tiles = seq // tm
        heads = pl.BlockSpec((None, WIDTH_A, tm), lambda i: (i // tiles, 0, i % tiles))
        heads_shape = jax.ShapeDtypeStruct((n // seq, WIDTH_A, seq), F32)
        qkv = pl.BlockSpec((WIDTH_A // LANES, tm, LANES), lambda i: (0, i, 0))
        qkv_shape = jax.ShapeDtypeStruct((WIDTH_A // LANES, n, LANES), F32)
    return pl.pallas_call(
        _ffn_proj_kernel,
        grid=(n // tm,),
        in_specs=[row(D_MODEL), _const_spec((1, D_MODEL)), _const_spec(wg.shape), _const_spec(wu.shape),
                  _const_spec(wd.shape), _const_spec((1, D_MODEL)), _const_spec(win.shape)],
        out_specs=[row(D_MODEL), qkv, qkv, qkv, row(WIDTH_B), heads, heads],
        out_shape=[jax.ShapeDtypeStruct((n, D_MODEL), F32)] + [qkv_shape] * 3
        + [jax.ShapeDtypeStruct((n, WIDTH_B), F32)] + [heads_shape] * 2,
        scratch_shapes=[pltpu.VMEM((tm, D_FF), BF16)],
        compiler_params=_params("parallel"),
        name="ffn_proj",
    )(x, g, wg, wu, wd, gmix, win)


def _ffn_final_kernel(x_ref, g_ref, wg_ref, wu_ref, wd_ref, gfin_ref, y_ref, act_ref):
    x1 = _swiglu_half_step(x_ref[...], g_ref, wg_ref, wu_ref, wd_ref, act_ref)
    y_ref[...] = _rms(x1, gfin_ref[...])


def _ffn_final(x, g, wg, wu, wd, gfin, tm):
    n = x.shape[0]
    row = pl.BlockSpec((tm, D_MODEL), lambda i: (i, 0))
    return pl.pallas_call(
        _ffn_final_kernel,
        grid=(n // tm,),
        in_specs=[row, _const_spec((1, D_MODEL)), _const_spec(wg.shape), _const_spec(wu.shape),
                  _const_spec(wd.shape), _const_spec((1, D_MODEL))],
        out_specs=row,
        out_shape=jax.ShapeDtypeStruct((n, D_MODEL), F32),
        scratch_shapes=[pltpu.VMEM((tm, D_FF), BF16)],
        compiler_params=_params("parallel"),
        name="ffn_final",
    )(x, g, wg, wu, wd, gfin)


def _attn_prompt_kernel(slopes_ref, q_ref, k_ref, v_ref, o_ref,
                        x4_ref, qs_ref, kp_ref, vp_ref, bias_ref, s_ref, p_ref, m_ref, l_ref, n_ref,
                        fm_ref, fl_ref, fn_ref, *, seq):
    pair = pl.program_id(0)
    nb = seq // BAND
    lo = lax.broadcasted_iota(jnp.int32, (BAND, LANES), 1) < HEAD_DIM

    @pl.when(pl.program_id(1) == 0)
    def _():
        kp_ref[:, 0:BAND, :] = jnp.zeros((3, BAND, LANES), BF16)
        vp_ref[:, 0:BAND, 0:LANES] = jnp.zeros((3, BAND, LANES), BF16)
        vp_ref[:, :, LANES:2 * LANES] = jnp.ones((3, BAND + seq, LANES), BF16)
        p_ref[...] = jnp.zeros(p_ref.shape, BF16)
        qi = lax.broadcasted_iota(jnp.int32, (BAND, 2 * BAND), 0)
        kj = lax.broadcasted_iota(jnp.int32, (BAND, 2 * BAND), 1)
        dist = qi + BAND - kj
        neg_dist = jnp.where((dist >= 0) & (dist <= BAND), -dist.astype(F32), NEG_BIG)
        neg_dist_cur = jnp.where(kj >= BAND, neg_dist, NEG_BIG)
        for b, (_, dil) in enumerate(DILATED_BRANCHES):
            for noprev in range(2):
                if 2 * b + noprev < bias_ref.shape[0]:
                    table = neg_dist_cur if noprev else neg_dist
                    for half in range(2):
                        bias_ref[2 * b + noprev, half * BAND:(half + 1) * BAND, :] = (
                            (table * slopes_ref[2 * pair + half]) * float(dil))

    def put(kind, b, first, x):
        nblocks = x.shape[0] // BAND
        if kind == 0:
            is_lo = lax.broadcasted_iota(jnp.int32, x.shape, 1) < HEAD_DIM
            x_lo = jnp.where(is_lo, x, 0.0).astype(BF16)
            x_hi = jnp.where(is_lo, 0.0, x).astype(BF16)
            for j in range(nblocks):
                qs_ref[b, first + j, 0] = x_lo[j * BAND:(j + 1) * BAND]
                qs_ref[b, first + j, 1] = x_hi[j * BAND:(j + 1) * BAND]
        elif kind == 1:
            kp_ref[b, BAND + first * BAND:BAND + first * BAND + x.shape[0], :] = x.astype(BF16)
        else:
            vp_ref[b, BAND + first * BAND:BAND + first * BAND + x.shape[0], 0:LANES] = x.astype(BF16)

    quarter = seq // 4
    for kind, ref in enumerate((q_ref, k_ref, v_ref)):
        for r in range(4):
            put(kind, 0, 4 * r, ref[r * quarter:(r + 1) * quarter, :])
            x = ref[pl.ds(r, quarter, stride=4), :]
            x4_ref[r * quarter:(r + 1) * quarter, :] = x
            put(kind, 1, 4 * r, x)
        for r in range(16):
            put(kind, 2, r, x4_ref[pl.ds((r % 4) * quarter + r // 4, seq // 16, stride=4), :])

    def blocks_per_class(b):
        return seq // DILATED_BRANCHES[b][1] // BAND

    def scores(n):
        out = []
        for b in range(3):
            qb = qs_ref[b, n].reshape(2 * BAND, LANES)
            if blocks_per_class(b) > 1:
                kb = kp_ref[b, pl.ds(pl.multiple_of(n * BAND, BAND), 2 * BAND), :]
                bias = bias_ref[2 * b + (n % blocks_per_class(b) == 0).astype(jnp.int32)]
            else:
                kb = kp_ref[b, pl.ds(pl.multiple_of(n * BAND + BAND, BAND), BAND), :]
                bias = bias_ref[2 * b, :, BAND:2 * BAND]
            out.append(_mm_nt(qb, kb) + bias)
        return out

    def keys_of(b):
        return 2 * BAND if blocks_per_class(b) > 1 else BAND

    def softmax(n):
        rows = pl.ds(pl.multiple_of(n * BAND, BAND), BAND)
        for b in range(3):
            s = s_ref[b, :, 0:keys_of(b)]
            m = jnp.max(s, axis=-1, keepdims=True)
            p_ref[b, :, 0:keys_of(b)] = jnp.exp(s - m).astype(BF16)
            m_ref[b, rows, :] = jnp.where(lo, m[0:BAND], m[BAND:2 * BAND])

    def weighted_values(n):
        out = []
        for b in range(3):
            first = n * BAND + (2 * BAND - keys_of(b))
            vb = vp_ref[b, pl.ds(pl.multiple_of(first, BAND), keys_of(b)), :]
            out.append(_mm(p_ref[b, :, 0:keys_of(b)], vb))
        return out

    def step(n):
        s_next = scores(jnp.minimum(n + 1, nb - 1))
        done = jnp.clip(n - 1, 0, nb - 1)
        pv = weighted_values(done)
        softmax(jnp.minimum(n, nb - 1))
        rows = pl.ds(pl.multiple_of(done * BAND, BAND), BAND)
        for b in range(3):
            s_ref[b, :, 0:keys_of(b)] = s_next[b]
            n_ref[b, rows, :] = jnp.where(lo, pv[b][0:BAND, 0:LANES], pv[b][BAND:2 * BAND, 0:LANES])
            l_ref[b, rows, :] = jnp.where(lo, pv[b][0:BAND, LANES:2 * LANES], pv[b][BAND:2 * BAND, LANES:2 * LANES])

    def trip(i, carry):
        for j in range(ATTN_UNROLL):
            step(i * ATTN_UNROLL + j)
        return carry

    for b, s0 in enumerate(scores(jnp.int32(0))):
        s_ref[b, :, 0:keys_of(b)] = s0
    lax.fori_loop(0, -(-(nb + 1) // ATTN_UNROLL), trip, 0)

    per16 = seq // 16
    for r in range(16):
        sl4 = pl.ds((r % 4) * quarter + r // 4, per16, stride=4)
        blk = pl.ds(r * per16, per16)
        m1, m2 = m_ref[1, sl4, :], m_ref[2, blk, :]
        mx = jnp.maximum(m1, m2)
        e1, e2 = jnp.exp(m1 - mx), jnp.exp(m2 - mx)
        fm_ref[sl4, :] = mx
        fl_ref[sl4, :] = e1 * l_ref[1, sl4, :] + e2 * l_ref[2, blk, :]
        fn_ref[sl4, :] = e1 * n_ref[1, sl4, :] + e2 * n_ref[2, blk, :]
    for r in range(4):
        for c in range(quarter // BAND):
            sl = pl.ds(r + 4 * BAND * c, BAND, stride=4)
            blk = pl.ds(r * quarter + c * BAND, BAND)
            m0, m1 = m_ref[0, sl, :], fm_ref[blk, :]
            mx = jnp.maximum(m0, m1)
            e0, e1 = jnp.exp(m0 - mx), jnp.exp(m1 - mx)
            den = e0 * l_ref[0, sl, :] + e1 * fl_ref[blk, :]
            num = e0 * n_ref[0, sl, :] + e1 * fn_ref[blk, :]
            o_ref[sl, :] = num / den


def _attn_prompt(q, k, v, slopes, batch, seq):
    assert [seq // d // BAND for _, d in DILATED_BRANCHES] == [16, 4, 1]
    nb = seq // BAND
    blk = pl.BlockSpec((None, seq, LANES), lambda p, b: (p, b, 0))
    return pl.pallas_call(
        functools.partial(_attn_prompt_kernel, seq=seq),
        grid=(WIDTH_A // LANES, batch),
        in_specs=[pl.BlockSpec(memory_space=pltpu.SMEM), blk, blk, blk],
        out_specs=blk,
        out_shape=jax.ShapeDtypeStruct((WIDTH_A // LANES, batch * seq, LANES), F32),
        scratch_shapes=[pltpu.VMEM((seq, LANES), F32),
                        pltpu.VMEM((3, nb, 2, BAND, LANES), BF16),
                        pltpu.VMEM((3, BAND + seq, LANES), BF16),
                        pltpu.VMEM((3, BAND + seq, 2 * LANES), BF16),
                        pltpu.VMEM((5, 2 * BAND, 2 * BAND), F32),
                        pltpu.VMEM((3, 2 * BAND, 2 * BAND), F32),
                        pltpu.VMEM((3, 2 * BAND, 2 * BAND), BF16)]
        + [pltpu.VMEM((3, seq, LANES), F32)] * 3 + [pltpu.VMEM((seq, LANES), F32)] * 3,
        compiler_params=_params("parallel", "arbitrary"),
        name="attn_prompt",
    )(slopes, q, k, v)


def _attn_sample_kernel(q_ref, kn_ref, vn_ref, kt_ref, vt_ref, o_ref, *, t_new, w_buf, pad):
    rows = N_HEADS_A * t_new
    zeros = jnp.zeros((pad - t_new, WIDTH_A), F32)
    k_new = jnp.concatenate([kn_ref[...], zeros], axis=0).astype(BF16)
    v_new = jnp.concatenate([vn_ref[...], zeros], axis=0).astype(BF16)

    qrep = jnp.concatenate([q_ref[...]] * N_HEADS_A, axis=0)
    rr = lax.broadcasted_iota(jnp.int32, (rows, WIDTH_A), 0)
    ll = lax.broadcasted_iota(jnp.int32, (rows, WIDTH_A), 1)
    qrows = jnp.where(rr // t_new == ll // HEAD_DIM, qrep, 0.0).astype(BF16)
    s = jnp.concatenate([_mm(qrows, kt_ref[...].astype(BF16)), _mm_nt(qrows, k_new)], axis=1)

    r2 = lax.broadcasted_iota(jnp.int32, s.shape, 0)
    u2 = lax.broadcasted_iota(jnp.int32, s.shape, 1)
    d = w_buf + r2 % t_new - u2
    mult = jnp.zeros(s.shape, F32)
    for window, dil in DILATED_BRANCHES:
        mult = mult + jnp.where((d >= 0) & (d <= window) & (d % dil == 0), 1.0, 0.0)
    head = lax.broadcasted_iota(jnp.int32, (rows, 1), 0) // t_new
    slope = jnp.zeros((rows, 1), F32)
    for h in range(N_HEADS_A):
        slope = jnp.where(head == h, 2.0 ** -(h + 1), slope)
    s = jnp.where(mult > 0, s - d.astype(F32) * slope, NEG_BIG)
    m = jnp.max(s, axis=-1, keepdims=True)
    p = mult * jnp.exp(s - m)
    den = jnp.sum(p, axis=-1, keepdims=True)
    p = p.astype(BF16)
    o = (_mm_nt(p[:, 0:w_buf], vt_ref[...].astype(BF16)) + _mm(p[:, w_buf:w_buf + pad], v_new)) * (1.0 / den)

    lane_head = lax.broadcasted_iota(jnp.int32, (t_new, WIDTH_A), 1) // HEAD_DIM
    out = jnp.zeros((t_new, WIDTH_A), F32)
    for h in range(N_HEADS_A):
        out = jnp.where(lane_head == h, o[h * t_new:(h + 1) * t_new, :], out)
    o_ref[...] = out


def _attn_sample(q, k, v, cache_kt, cache_vt, t_new):
    batch, _, w_buf = cache_kt.shape
    assert t_new % SUBLANES == 0 and w_buf >= DILATED_BRANCHES[-1][0] and w_buf % LANES == 0
    pad = LANES
    new = pl.BlockSpec((t_new, WIDTH_A), lambda b: (b, 0))
    cache = pl.BlockSpec((None, WIDTH_A, w_buf), lambda b: (b, 0, 0))
    return pl.pallas_call(
        functools.partial(_attn_sample_kernel, t_new=t_new, w_buf=w_buf, pad=pad),
        grid=(batch,),
        in_specs=[new, new, new, cache, cache],
        out_specs=new,
        out_shape=jax.ShapeDtypeStruct((batch * t_new, WIDTH_A), F32),
        compiler_params=_params("parallel"),
        name="attn_sample",
    )(q, k, v, cache_kt, cache_vt)


def _mem_kv_kernel(mem_ref, g_ref, wk_ref, wv_ref, mk_ref, mv_ref, mkh_ref, mvh_ref):
    h = _rms(mem_ref[...], g_ref[...]).astype(BF16)
    mk = _mm(h, wk_ref[...])
    mv = _mm(h, wv_ref[...])
    mk_ref[...] = mk.astype(BF16)
    mv_ref[...] = mv.astype(BF16)
    mkh_ref[...] = mk.reshape(mkh_ref.shape)
    mvh_ref[...] = mv.reshape(mvh_ref.shape)


def _mem_kv(mem, g, wk, wv, tm):
    n = mem.shape[0]
    row = pl.BlockSpec((tm, D_MODEL), lambda i: (i, 0))
    heads = pl.BlockSpec((tm, N_HEADS_X, HEAD_DIM_X), lambda i: (i, 0, 0))
    return pl.pallas_call(
        _mem_kv_kernel,
        grid=(n // tm,),
        in_specs=[row, _const_spec((1, D_MODEL)), _const_spec(wk.shape), _const_spec(wv.shape)],
        out_specs=[row, row, heads, heads],
        out_shape=[jax.ShapeDtypeStruct((n, D_MODEL), BF16)] * 2
        + [jax.ShapeDtypeStruct((n, N_HEADS_X, HEAD_DIM_X), F32)] * 2,
        compiler_params=_params("parallel"),
        name="mem_kv",
    )(mem, g, wk, wv)


def _shifted_rows(first_row, rows):
    offs = [first_row + j for j in range(CONV_WIDTH) if (first_row + j) % SUBLANES]
    return max(offs) // SUBLANES * SUBLANES + rows


def _realign_conv_input(ext_ref, sh_ref):
    for s in range(1, SUBLANES):
        sh_ref[s - 1] = ext_ref[pl.ds(s, sh_ref.shape[1]), :]


def _conv_module(ext_ref, sh_ref, first_row, row0, rows, cw_ref, cb_ref, lg_ref, lb_ref):
    y = cb_ref[...]
    for j in range(CONV_WIDTH):
        base, s = (first_row + j) // SUBLANES * SUBLANES, (first_row + j) % SUBLANES
        at = pl.ds(base + row0, rows)
        y = y + cw_ref[j:j + 1, :] * (ext_ref[at, :] if s == 0 else sh_ref[s - 1, at, :])
    yc = y - jnp.mean(y, axis=-1, keepdims=True)
    yn = yc * lax.rsqrt(jnp.mean(yc * yc, axis=-1, keepdims=True) + EPS) * lg_ref[...] + lb_ref[...]
    return yn * jax.nn.sigmoid(yn)


def _softmax_rows(s):
    m = jnp.max(s, axis=-1, keepdims=True)
    p = jnp.exp(s - m)
    return p.astype(BF16), 1.0 / jnp.sum(p, axis=-1, keepdims=True)


def _mix_prompt_kernel(x_ref, u_ref, halo_ref, oa_ref, mk_ref, mv_ref, cw_ref, cb_ref, lg_ref, lb_ref,
                       wout_ref, gx_ref, wq_ref, wo_ref, y_ref, ext_ref, sh_ref):
    ts = x_ref.shape[0]
    halo = halo_ref[...]
    ext_ref[0:HALO, :] = jnp.where(pl.program_id(1) == 0, jnp.zeros_like(halo), halo)
    ext_ref[HALO:HALO + ts, :] = u_ref[...]
    _realign_conv_input(ext_ref, sh_ref)
    ob = _conv_module(ext_ref, sh_ref, HALO - (CONV_WIDTH - 1), 0, ts, cw_ref, cb_ref, lg_ref, lb_ref)
    oa = jnp.concatenate([oa_ref[p] for p in range(oa_ref.shape[0])], axis=-1)
    x2 = (x_ref[...] + _mm(oa.astype(BF16), wout_ref[0:WIDTH_A, :])
          + _mm(ob.astype(BF16), wout_ref[WIDTH_A:WIDTH_A + WIDTH_B, :]))
    hx = _rms(x2, gx_ref[...]).astype(BF16)
    qx = (_mm(hx, wq_ref[...]) * (HEAD_DIM_X ** -0.5)).astype(BF16)
    outs = []
    for h in range(N_HEADS_X):
        sl = slice(h * HEAD_DIM_X, (h + 1) * HEAD_DIM_X)
        p, inv = _softmax_rows(_mm_nt(qx[:, sl], mk_ref[:, sl]))
        outs.append((_mm(p, mv_ref[:, sl]) * inv).astype(BF16))
    y_ref[...] = x2 + _mm(jnp.concatenate(outs, axis=-1), wo_ref[...])


def _mix_prompt(x, u, oa, mk, mv, cw, cb, lg, lb, wout, gx, wq, wo, batch, seq, ts):
    tiles = seq // ts
    row = lambda w: pl.BlockSpec((ts, w), lambda b, i: (b * tiles + i, 0))
    oa_spec = pl.BlockSpec((WIDTH_A // LANES, ts, LANES), lambda b, i: (0, b * tiles + i, 0))
    halo = pl.BlockSpec((HALO, WIDTH_B),
                        lambda b, i: (jnp.maximum((b * seq + i * ts) // HALO - 1, 0), 0))
    mem = pl.BlockSpec((N_MEM, D_MODEL), lambda b, i: (b, 0))
    vec = lambda w: _const_spec((1, w))
    return pl.pallas_call(
        _mix_prompt_kernel,
        grid=(batch, tiles),
        in_specs=[row(D_MODEL), row(WIDTH_B), halo, oa_spec, mem, mem, _const_spec(cw.shape),
                  vec(WIDTH_B), vec(WIDTH_B), vec(WIDTH_B), _const_spec(wout.shape), vec(D_MODEL),
                  _const_spec(wq.shape), _const_spec(wo.shape)],
        out_specs=row(D_MODEL),
        out_shape=jax.ShapeDtypeStruct((batch * seq, D_MODEL), F32),
        scratch_shapes=[pltpu.VMEM((HALO + ts, WIDTH_B), F32),
                        pltpu.VMEM((SUBLANES - 1, _shifted_rows(HALO - (CONV_WIDTH - 1), ts), WIDTH_B), F32)],
        compiler_params=_params("parallel", "parallel"),
        name="mix_prompt",
    )(x, u, u, oa, mk, mv, cw, cb, lg, lb, wout, gx, wq, wo)


def _mix_sample_kernel(x_ref, uext_ref, oa_ref, mk_ref, mv_ref, cw_ref, cb_ref, lg_ref, lb_ref,
                       wout_ref, gx_ref, wq_ref, wo_ref, y_ref, sh_ref, *, group, t_new, t_ext):
    rows = group * t_ext
    _realign_conv_input(uext_ref, sh_ref)
    conv = _conv_module(uext_ref, sh_ref, 0, 0, rows, cw_ref, cb_ref, lg_ref, lb_ref)
    ob = conv.reshape(group, t_ext, WIDTH_B)[:, 0:t_new, :].reshape(group * t_new, WIDTH_B)
    x2 = (x_ref[...] + _mm(oa_ref[...].astype(BF16), wout_ref[0:WIDTH_A, :])
          + _mm(ob.astype(BF16), wout_ref[WIDTH_A:WIDTH_A + WIDTH_B, :]))

    hx = _rms(x2, gx_ref[...]).astype(BF16)
    qx = _mm(hx, wq_ref[...]) * (HEAD_DIM_X ** -0.5)
    qrows_n = N_HEADS_X * t_new
    rr = lax.broadcasted_iota(jnp.int32, (qrows_n, D_MODEL), 0)
    ll = lax.broadcasted_iota(jnp.int32, (qrows_n, D_MODEL), 1)
    own = rr // t_new == ll // HEAD_DIM_X
    lane_head = lax.broadcasted_iota(jnp.int32, (t_new, D_MODEL), 1) // HEAD_DIM_X
    outs = []
    for b in range(group):
        qb = qx[b * t_new:(b + 1) * t_new, :]
        qrows = jnp.where(own, jnp.concatenate([qb] * N_HEADS_X, axis=0), 0.0).astype(BF16)
        mk = mk_ref[b].reshape(N_MEM, D_MODEL).astype(BF16)
        mv = mv_ref[b].reshape(N_MEM, D_MODEL).astype(BF16)
        p, inv = _softmax_rows(_mm_nt(qrows, mk))
        o = _mm(p, mv) * inv
        out = jnp.zeros((t_new, D_MODEL), F32)
        for h in range(N_HEADS_X):
            out = jnp.where(lane_head == h, o[h * t_new:(h + 1) * t_new, :], out)
        outs.append(out)
    y_ref[...] = x2 + _mm(jnp.concatenate(outs, axis=0).astype(BF16), wo_ref[...])


def _mix_sample(x, uext, oa, mk, mv, cw, cb, lg, lb, wout, gx, wq, wo, batch, t_new, t_ext, group):
    rows = group * t_new
    row = lambda w: pl.BlockSpec((rows, w), lambda i: (i, 0))
    ext = pl.BlockSpec((group * t_ext + HALO, WIDTH_B), lambda i: (i, 0))
    mem = pl.BlockSpec((group, N_MEM, N_HEADS_X, HEAD_DIM_X), lambda i: (i, 0, 0, 0))
    vec = lambda w: _const_spec((1, w))
    return pl.pallas_call(
        functools.partial(_mix_sample_kernel, group=group, t_new=t_new, t_ext=t_ext),
        grid=(batch // group,),
        in_specs=[row(D_MODEL), ext, row(WIDTH_A), mem, mem, _const_spec(cw.shape),
                  vec(WIDTH_B), vec(WIDTH_B), vec(WIDTH_B), _const_spec(wout.shape), vec(D_MODEL),
                  _const_spec(wq.shape), _const_spec(wo.shape)],
        out_specs=row(D_MODEL),
        out_shape=jax.ShapeDtypeStruct((batch * t_new, D_MODEL), F32),
        scratch_shapes=[pltpu.VMEM((SUBLANES - 1, _shifted_rows(0, group * t_ext), WIDTH_B), F32)],
        compiler_params=_params("parallel"),
        name="mix_sample",
    )(x, uext, oa, mk, mv, cw, cb, lg, lb, wout, gx, wq, wo)


def kernel(x_prompt, x_sample, mem_prompt, cache_win_k, cache_win_v, cache_conv, cache_mem_k, cache_mem_v, ffn1_norm, ffn1_gate, ffn1_up, ffn1_down, mix_norm, w_in, conv_w, conv_b, conv_ln_g, conv_ln_b, w_out, xattn_norm, mem_norm, w_cq, w_ck, w_cv, w_co, ffn2_norm, ffn2_gate, ffn2_up, ffn2_down, final_norm):
    depth = ffn1_norm.shape[0]
    assert depth == 1
    bp, seq, _ = x_prompt.shape
    bs, t_new, _ = x_sample.shape
    keep = CONV_WIDTH - 1
    l = 0
    vec = lambda a: a.reshape(1, -1)
    bf = lambda a: a.astype(BF16)
    slopes = jnp.asarray([2.0 ** -(h + 1) for h in range(N_HEADS_A)], F32)

    f1 = (vec(ffn1_norm[l]), bf(ffn1_gate[l]), bf(ffn1_up[l]), bf(ffn1_down[l]))
    f2 = (vec(ffn2_norm[l]), bf(ffn2_gate[l]), bf(ffn2_up[l]), bf(ffn2_down[l]))
    gmix, win = vec(mix_norm[l]), bf(w_in[l])
    conv = (conv_w[l], vec(conv_b[l]), vec(conv_ln_g[l]), vec(conv_ln_b[l]))
    proj = (bf(w_out[l]), vec(xattn_norm[l]), bf(w_cq[l]), bf(w_co[l]))
    gfin = vec(final_norm)

    xp = x_prompt.reshape(bp * seq, D_MODEL)
    x1, q, k, v, u, kt, vt = _ffn_proj(xp, *f1, gmix, win, tm=512, seq=seq)
    oa = _attn_prompt(q, k, v, slopes, bp, seq)
    mk, mv, mkh, mvh = _mem_kv(mem_prompt.reshape(bp * N_MEM, D_MODEL), vec(mem_norm[l]), bf(w_ck[l]), bf(w_cv[l]),
                               tm=512)
    x3 = _mix_prompt(x1, u, oa, mk, mv, *conv, *proj, batch=bp, seq=seq, ts=512)
    yp = _ffn_final(x3, *f2, gfin, tm=1024)

    xs = x_sample.reshape(bs * t_new, D_MODEL)
    s1, sq, sk, sv, su, skh, svh = _ffn_proj(xs, *f1, gmix, win, tm=bs * t_new)
    seq_minor = lambda c: jnp.transpose(c, (0, 2, 3, 1)).reshape(c.shape[0], WIDTH_A, c.shape[1])
    soa = _attn_sample(sq, sk, sv, seq_minor(cache_win_k[l]), seq_minor(cache_win_v[l]), t_new)
    u_ext = jnp.concatenate([cache_conv[l], su.reshape(bs, t_new, WIDTH_B)], axis=1)
    t_ext = -(-(keep + t_new) // 8) * 8
    group = 8
    u_pad = jnp.pad(u_ext, ((0, 0), (0, t_ext - keep - t_new), (0, 0))).reshape(bs // group, group * t_ext, WIDTH_B)
    u_pad = jnp.pad(u_pad, ((0, 0), (0, HALO), (0, 0))).reshape(-1, WIDTH_B)
    s3 = _mix_sample(s1, u_pad, soa, cache_mem_k[l], cache_mem_v[l], *conv, *proj,
                     batch=bs, t_new=t_new, t_ext=t_ext, group=group)
    ys = _ffn_final(s3, *f2, gfin, tm=bs * t_new)

    heads = lambda a, b, t: a.reshape(1, b, t, N_HEADS_A, HEAD_DIM)
    from_seq_minor = lambda a: jnp.transpose(a.reshape(1, bp, N_HEADS_A, HEAD_DIM, seq), (0, 1, 4, 2, 3))
    mem_heads = lambda a: a.reshape(1, bp, N_MEM, N_HEADS_X, HEAD_DIM_X)
    return (yp.reshape(bp, seq, D_MODEL),
            ys.reshape(bs, t_new, D_MODEL),
            from_seq_minor(kt), from_seq_minor(vt),
            u.reshape(1, bp, seq, WIDTH_B)[:, :, seq - keep:],
            mem_heads(mkh), mem_heads(mvh),
            heads(skh, bs, t_new), heads(svh, bs, t_new),
            u_ext[None, :, t_new:])
```

```python
import functools

import jax
import jax.numpy as jnp
from jax import lax
from jax.experimental import pallas as pl
from jax.experimental.pallas import tpu as pltpu

D_MODEL = 1024
HEAD_DIM = 64
N_HEADS_A = 8
WIDTH_A = N_HEADS_A * HEAD_DIM
WIDTH_B = D_MODEL - WIDTH_A
DILATED_BRANCHES = ((128, 1), (512, 4), (2048, 16))
BAND = 128
CONV_WIDTH = 31
D_FF = 2816
N_MEM = 256
N_HEADS_X = 4
HEAD_DIM_X = D_MODEL // N_HEADS_X
EPS = 1e-6

LANES = 128
SUBLANES = 8
FF_CHUNK = 256
HALO = 32
ATTN_GROUP = 2
NEG_BIG = -1e30
VMEM_LIMIT = 56 * 1024 * 1024

F32 = jnp.float32
BF16 = jnp.bfloat16


def _const_spec(shape):
    nd = len(shape)
    return pl.BlockSpec(shape, lambda *_: (0,) * nd, pipeline_mode=pl.Buffered(1))


def _params(*sem):
    return pltpu.CompilerParams(dimension_semantics=sem, vmem_limit_bytes=VMEM_LIMIT)


def _rms(x, g):
    return x * lax.rsqrt(jnp.mean(x * x, axis=-1, keepdims=True) + EPS) * g


def _mm(a, b):
    return jnp.dot(a, b, preferred_element_type=F32)


def _mm_nt(a, b):
    return lax.dot_general(a, b, (((1,), (1,)), ((), ())), preferred_element_type=F32)


def _swiglu_half_step(x, g_ref, wg_ref, wu_ref, wd_ref, act_ref):
    h = _rms(x, g_ref[...]).astype(BF16)
    for c in range(0, D_FF, FF_CHUNK):
        gate = _mm(h, wg_ref[:, c:c + FF_CHUNK])
        up = _mm(h, wu_ref[:, c:c + FF_CHUNK])
        act_ref[:, c:c + FF_CHUNK] = (gate * jax.nn.sigmoid(gate) * up).astype(BF16)
    return x + 0.5 * _mm(act_ref[...], wd_ref[...])


def _ffn_proj_kernel(x_ref, g_ref, wg_ref, wu_ref, wd_ref, gmix_ref, win_ref,
                     x1_ref, q_ref, k_ref, v_ref, u_ref, kh_ref, vh_ref, act_ref):
    def put_cols(ref, val):
        if len(ref.shape) == 2:
            ref[...] = val
        else:
            for p in range(ref.shape[0]):
                ref[p] = val[:, p * LANES:(p + 1) * LANES]

    x1 = _swiglu_half_step(x_ref[...], g_ref, wg_ref, wu_ref, wd_ref, act_ref)
    x1_ref[...] = x1
    h = _rms(x1, gmix_ref[...]).astype(BF16)
    w = WIDTH_A
    put_cols(q_ref, _mm(h, win_ref[:, 0:w]) * (HEAD_DIM ** -0.5))
    k = _mm(h, win_ref[:, w:2 * w])
    v = _mm(h, win_ref[:, 2 * w:3 * w])
    put_cols(k_ref, k)
    put_cols(v_ref, v)
    if len(kh_ref.shape) == 2:
        kh_ref[...] = k.T
        vh_ref[...] = v.T
    else:
        kh_ref[...] = k.reshape(kh_ref.shape)
        vh_ref[...] = v.reshape(vh_ref.shape)
    a = _mm(h, win_ref[:, 3 * w:3 * w + WIDTH_B])
    g = _mm(h, win_ref[:, 3 * w + WIDTH_B:3 * w + 2 * WIDTH_B])
    u_ref[...] = a * jax.nn.sigmoid(g)


def _ffn_proj(x, g, wg, wu, wd, gmix, win, tm, seq=None):
    n = x.shape[0]
    row = lambda w: pl.BlockSpec((tm, w), lambda i: (i, 0))
    if seq is None:
        heads = pl.BlockSpec((tm, N_HEADS_A, HEAD_DIM), lambda i: (i, 0, 0))
        heads_shape = jax.ShapeDtypeStruct((n, N_HEADS_A, HEAD_DIM), F32)
        qkv = row(WIDTH_A)
        qkv_shape = jax.ShapeDtypeStruct((n, WIDTH_A), F32)
    else:
        tiles = seq // tm
        heads = pl.BlockSpec((None, WIDTH_A, tm), lambda i: (i // tiles, 0, i % tiles))
        heads_shape = jax.ShapeDtypeStruct((n // seq, WIDTH_A, seq), F32)
        qkv = pl.BlockSpec((WIDTH_A // LANES, tm, LANES), lambda i: (0, i, 0))
        qkv_shape = jax.ShapeDtypeStruct((WIDTH_A // LANES, n, LANES), F32)
    return pl.pallas_call(
        _ffn_proj_kernel,
        grid=(n // tm,),
        in_specs=[row(D_MODEL), _const_spec((1, D_MODEL)), _const_spec(wg.shape), _const_spec(wu.shape),
                  _const_spec(wd.shape), _const_spec((1, D_MODEL)), _const_spec(win.shape)],
        out_specs=[row(D_MODEL), qkv, qkv, qkv, row(WIDTH_B), heads, heads],
        out_shape=[jax.ShapeDtypeStruct((n, D_MODEL), F32)] + [qkv_shape] * 3
        + [jax.ShapeDtypeStruct((n, WIDTH_B), F32)] + [heads_shape] * 2,
        scratch_shapes=[pltpu.VMEM((tm, D_FF), BF16)],
        compiler_params=_params("parallel"),
        name="ffn_proj",
    )(x, g, wg, wu, wd, gmix, win)


def _ffn_final_kernel(x_ref, g_ref, wg_ref, wu_ref, wd_ref, gfin_ref, y_ref, act_ref):
    x1 = _swiglu_half_step(x_ref[...], g_ref, wg_ref, wu_ref, wd_ref, act_ref)
    y_ref[...] = _rms(x1, gfin_ref[...])


def _ffn_final(x, g, wg, wu, wd, gfin, tm):
    n = x.shape[0]
    row = pl.BlockSpec((tm, D_MODEL), lambda i: (i, 0))
    return pl.pallas_call(
        _ffn_final_kernel,
        grid=(n // tm,),
        in_specs=[row, _const_spec((1, D_MODEL)), _const_spec(wg.shape), _const_spec(wu.shape),
                  _const_spec(wd.shape), _const_spec((1, D_MODEL))],
        out_specs=row,
        out_shape=jax.ShapeDtypeStruct((n, D_MODEL), F32),
        scratch_shapes=[pltpu.VMEM((tm, D_FF), BF16)],
        compiler_params=_params("parallel"),
        name="ffn_final",
    )(x, g, wg, wu, wd, gfin)


def _attn_prompt_kernel(slopes_ref, q_ref, k_ref, v_ref, o_ref,
                        x4_ref, qs_ref, kp_ref, vp_ref, bias_ref, s_ref, p_ref, m_ref, l_ref, n_ref,
                        fm_ref, fl_ref, fn_ref, *, seq):
    pair = pl.program_id(0)
    nb = seq // BAND
    lo = lax.broadcasted_iota(jnp.int32, (BAND, LANES), 1) < HEAD_DIM

    @pl.when(pl.program_id(1) == 0)
    def _():
        kp_ref[:, 0:BAND, :] = jnp.zeros((3, BAND, LANES), BF16)
        vp_ref[:, 0:BAND, 0:LANES] = jnp.zeros((3, BAND, LANES), BF16)
        vp_ref[:, :, LANES:2 * LANES] = jnp.ones((3, BAND + seq, LANES), BF16)
        p_ref[...] = jnp.zeros(p_ref.shape, BF16)
        qi = lax.broadcasted_iota(jnp.int32, (BAND, 2 * BAND), 0)
        kj = lax.broadcasted_iota(jnp.int32, (BAND, 2 * BAND), 1)
        dist = qi + BAND - kj
        neg_dist = jnp.where((dist >= 0) & (dist <= BAND), -dist.astype(F32), NEG_BIG)
        neg_dist_cur = jnp.where(kj >= BAND, neg_dist, NEG_BIG)
        for b, (_, dil) in enumerate(DILATED_BRANCHES):
            for noprev in range(2):
                if 2 * b + noprev < bias_ref.shape[0]:
                    table = neg_dist_cur if noprev else neg_dist
                    for half in range(2):
                        bias_ref[2 * b + noprev, half * BAND:(half + 1) * BAND, :] = (
                            (table * slopes_ref[2 * pair + half]) * float(dil))

    def put(kind, b, first, x):
        nblocks = x.shape[0] // BAND
        if kind == 0:
            is_lo = lax.broadcasted_iota(jnp.int32, x.shape, 1) < HEAD_DIM
            x_lo = jnp.where(is_lo, x, 0.0).astype(BF16)
            x_hi = jnp.where(is_lo, 0.0, x).astype(BF16)
            for j in range(nblocks):
                qs_ref[b, first + j, 0] = x_lo[j * BAND:(j + 1) * BAND]
                qs_ref[b, first + j, 1] = x_hi[j * BAND:(j + 1) * BAND]
        elif kind == 1:
            kp_ref[b, BAND + first * BAND:BAND + first * BAND + x.shape[0], :] = x.astype(BF16)
        else:
            vp_ref[b, BAND + first * BAND:BAND + first * BAND + x.shape[0], 0:LANES] = x.astype(BF16)

    quarter = seq // 4
    for kind, ref in enumerate((q_ref, k_ref, v_ref)):
        for r in range(4):
            put(kind, 0, 4 * r, ref[r * quarter:(r + 1) * quarter, :])
            x = ref[pl.ds(r, quarter, stride=4), :]
            x4_ref[r * quarter:(r + 1) * quarter, :] = x
            put(kind, 1, 4 * r, x)
        for r in range(16):
            put(kind, 2, r, x4_ref[pl.ds((r % 4) * quarter + r // 4, seq // 16, stride=4), :])

    def blocks_per_class(b):
        return seq // DILATED_BRANCHES[b][1] // BAND

    def scores(n):
        out = []
        for b in range(3):
            qb = qs_ref[b, n].reshape(2 * BAND, LANES)
            if blocks_per_class(b) > 1:
                kb = kp_ref[b, pl.ds(pl.multiple_of(n * BAND, BAND), 2 * BAND), :]
                bias = bias_ref[2 * b + (n % blocks_per_class(b) == 0).astype(jnp.int32)]
            else:
                kb = kp_ref[b, pl.ds(pl.multiple_of(n * BAND + BAND, BAND), BAND), :]
                bias = bias_ref[2 * b, :, BAND:2 * BAND]
            out.append(_mm_nt(qb, kb) + bias)
        return out

    def keys_of(b):
        return 2 * BAND if blocks_per_class(b) > 1 else BAND

    def softmax(g, n):
        rows = pl.ds(pl.multiple_of(n * BAND, BAND), BAND)
        for b in range(3):
            s = s_ref[g, b, :, 0:keys_of(b)]
            m = jnp.max(s, axis=-1, keepdims=True)
            p_ref[g, b, :, 0:keys_of(b)] = jnp.exp(s - m).astype(BF16)
            m_ref[b, rows, :] = jnp.where(lo, m[0:BAND], m[BAND:2 * BAND])

    def weighted_values(g, n):
        out = []
        for b in range(3):
            first = n * BAND + (2 * BAND - keys_of(b))
            vb = vp_ref[b, pl.ds(pl.multiple_of(first, BAND), keys_of(b)), :]
            out.append(_mm(p_ref[g, b, :, 0:keys_of(b)], vb))
        return out

    groups = nb // ATTN_GROUP

    def step(i, carry):
        ahead, cur, done = jnp.minimum(i + 1, groups - 1), jnp.minimum(i, groups - 1), jnp.clip(i - 1, 0, groups - 1)
        pv = [weighted_values(g, done * ATTN_GROUP + g) for g in range(ATTN_GROUP)]
        s_next = [scores(ahead * ATTN_GROUP + g) for g in range(ATTN_GROUP)]
        for g in range(ATTN_GROUP):
            softmax(g, cur * ATTN_GROUP + g)
        for g in range(ATTN_GROUP):
            rows = pl.ds(pl.multiple_of((done * ATTN_GROUP + g) * BAND, BAND), BAND)
            for b in range(3):
                s_ref[g, b, :, 0:keys_of(b)] = s_next[g][b]
                n_ref[b, rows, :] = jnp.where(lo, pv[g][b][0:BAND, 0:LANES], pv[g][b][BAND:2 * BAND, 0:LANES])
                l_ref[b, rows, :] = jnp.where(lo, pv[g][b][0:BAND, LANES:2 * LANES],
                                              pv[g][b][BAND:2 * BAND, LANES:2 * LANES])
        return carry

    for g in range(ATTN_GROUP):
        for b, s0 in enumerate(scores(jnp.int32(g))):
            s_ref[g, b, :, 0:keys_of(b)] = s0
    lax.fori_loop(0, groups + 1, step, 0)

    per16 = seq // 16
    for r in range(16):
        sl4 = pl.ds((r % 4) * quarter + r // 4, per16, stride=4)
        blk = pl.ds(r * per16, per16)
        m1, m2 = m_ref[1, sl4, :], m_ref[2, blk, :]
        mx = jnp.maximum(m1, m2)
        e1, e2 = jnp.exp(m1 - mx), jnp.exp(m2 - mx)
        fm_ref[sl4, :] = mx
        fl_ref[sl4, :] = e1 * l_ref[1, sl4, :] + e2 * l_ref[2, blk, :]
        fn_ref[sl4, :] = e1 * n_ref[1, sl4, :] + e2 * n_ref[2, blk, :]
    for r in range(4):
        for c in range(quarter // BAND):
            sl = pl.ds(r + 4 * BAND * c, BAND, stride=4)
            blk = pl.ds(r * quarter + c * BAND, BAND)
            m0, m1 = m_ref[0, sl, :], fm_ref[blk, :]
            mx = jnp.maximum(m0, m1)
            e0, e1 = jnp.exp(m0 - mx), jnp.exp(m1 - mx)
            den = e0 * l_ref[0, sl, :] + e1 * fl_ref[blk, :]
            num = e0 * n_ref[0, sl, :] + e1 * fn_ref[blk, :]
            o_ref[sl, :] = num / den


def _attn_prompt(q, k, v, slopes, batch, seq):
    assert [seq // d // BAND for _, d in DILATED_BRANCHES] == [16, 4, 1]
    nb = seq // BAND
    blk = pl.BlockSpec((None, seq, LANES), lambda p, b: (p, b, 0))
    return pl.pallas_call(
        functools.partial(_attn_prompt_kernel, seq=seq),
        grid=(WIDTH_A // LANES, batch),
        in_specs=[pl.BlockSpec(memory_space=pltpu.SMEM), blk, blk, blk],
        out_specs=blk,
        out_shape=jax.ShapeDtypeStruct((WIDTH_A // LANES, batch * seq, LANES), F32),
        scratch_shapes=[pltpu.VMEM((seq, LANES), F32),
                        pltpu.VMEM((3, nb, 2, BAND, LANES), BF16),
                        pltpu.VMEM((3, BAND + seq, LANES), BF16),
                        pltpu.VMEM((3, BAND + seq, 2 * LANES), BF16),
                        pltpu.VMEM((5, 2 * BAND, 2 * BAND), F32),
                        pltpu.VMEM((ATTN_GROUP, 3, 2 * BAND, 2 * BAND), F32),
                        pltpu.VMEM((ATTN_GROUP, 3, 2 * BAND, 2 * BAND), BF16)]
        + [pltpu.VMEM((3, seq, LANES), F32)] * 3 + [pltpu.VMEM((seq, LANES), F32)] * 3,
        compiler_params=_params("parallel", "arbitrary"),
        name="attn_prompt",
    )(slopes, q, k, v)


def _attn_sample_kernel(q_ref, kn_ref, vn_ref, kt_ref, vt_ref, o_ref, *, t_new, w_buf, pad):
    rows = N_HEADS_A * t_new
    zeros = jnp.zeros((pad - t_new, WIDTH_A), F32)
    k_new = jnp.concatenate([kn_ref[...], zeros], axis=0).astype(BF16)
    v_new = jnp.concatenate([vn_ref[...], zeros], axis=0).astype(BF16)

    qrep = jnp.concatenate([q_ref[...]] * N_HEADS_A, axis=0)
    rr = lax.broadcasted_iota(jnp.int32, (rows, WIDTH_A), 0)
    ll = lax.broadcasted_iota(jnp.int32, (rows, WIDTH_A), 1)
    qrows = jnp.where(rr // t_new == ll // HEAD_DIM, qrep, 0.0).astype(BF16)
    s = jnp.concatenate([_mm(qrows, kt_ref[...].astype(BF16)), _mm_nt(qrows, k_new)], axis=1)

    r2 = lax.broadcasted_iota(jnp.int32, s.shape, 0)
    u2 = lax.broadcasted_iota(jnp.int32, s.shape, 1)
    d = w_buf + r2 % t_new - u2
    mult = jnp.zeros(s.shape, F32)
    for window, dil in DILATED_BRANCHES:
        mult = mult + jnp.where((d >= 0) & (d <= window) & (d % dil == 0), 1.0, 0.0)
    head = lax.broadcasted_iota(jnp.int32, (rows, 1), 0) // t_new
    slope = jnp.zeros((rows, 1), F32)
    for h in range(N_HEADS_A):
        slope = jnp.where(head == h, 2.0 ** -(h + 1), slope)
    s = jnp.where(mult > 0, s - d.astype(F32) * slope, NEG_BIG)
    m = jnp.max(s, axis=-1, keepdims=True)
    p = mult * jnp.exp(s - m)
    den = jnp.sum(p, axis=-1, keepdims=True)
    p = p.astype(BF16)
    o = (_mm_nt(p[:, 0:w_buf], vt_ref[...].astype(BF16)) + _mm(p[:, w_buf:w_buf + pad], v_new)) * (1.0 / den)

    lane_head = lax.broadcasted_iota(jnp.int32, (t_new, WIDTH_A), 1) // HEAD_DIM
    out = jnp.zeros((t_new, WIDTH_A), F32)
    for h in range(N_HEADS_A):
        out = jnp.where(lane_head == h, o[h * t_new:(h + 1) * t_new, :], out)
    o_ref[...] = out


def _attn_sample(q, k, v, cache_kt, cache_vt, t_new):
    batch, _, w_buf = cache_kt.shape
    assert t_new % SUBLANES == 0 and w_buf >= DILATED_BRANCHES[-1][0] and w_buf % LANES == 0
    pad = LANES
    new = pl.BlockSpec((t_new, WIDTH_A), lambda b: (b, 0))
    cache = pl.BlockSpec((None, WIDTH_A, w_buf), lambda b: (b, 0, 0))
    return pl.pallas_call(
        functools.partial(_attn_sample_kernel, t_new=t_new, w_buf=w_buf, pad=pad),
        grid=(batch,),
        in_specs=[new, new, new, cache, cache],
        out_specs=new,
        out_shape=jax.ShapeDtypeStruct((batch * t_new, WIDTH_A), F32),
        compiler_params=_params("parallel"),
        name="attn_sample",
    )(q, k, v, cache_kt, cache_vt)


def _mem_kv_kernel(mem_ref, g_ref, wk_ref, wv_ref, mk_ref, mv_ref, mkh_ref, mvh_ref):
    h = _rms(mem_ref[...], g_ref[...]).astype(BF16)
    mk = _mm(h, wk_ref[...])
    mv = _mm(h, wv_ref[...])
    mk_ref[...] = mk.astype(BF16)
    mv_ref[...] = mv.astype(BF16)
    mkh_ref[...] = mk.reshape(mkh_ref.shape)
    mvh_ref[...] = mv.reshape(mvh_ref.shape)


def _mem_kv(mem, g, wk, wv, tm):
    n = mem.shape[0]
    row = pl.BlockSpec((tm, D_MODEL), lambda i: (i, 0))
    heads = pl.BlockSpec((tm, N_HEADS_X, HEAD_DIM_X), lambda i: (i, 0, 0))
    return pl.pallas_call(
        _mem_kv_kernel,
        grid=(n // tm,),
        in_specs=[row, _const_spec((1, D_MODEL)), _const_spec(wk.shape), _const_spec(wv.shape)],
        out_specs=[row, row, heads, heads],
        out_shape=[jax.ShapeDtypeStruct((n, D_MODEL), BF16)] * 2
        + [jax.ShapeDtypeStruct((n, N_HEADS_X, HEAD_DIM_X), F32)] * 2,
        compiler_params=_params("parallel"),
        name="mem_kv",
    )(mem, g, wk, wv)


def _shifted_rows(first_row, rows):
    offs = [first_row + j for j in range(CONV_WIDTH) if (first_row + j) % SUBLANES]
    return max(offs) // SUBLANES * SUBLANES + rows


def _realign_conv_input(ext_ref, sh_ref):
    for s in range(1, SUBLANES):
        sh_ref[s - 1] = ext_ref[pl.ds(s, sh_ref.shape[1]), :]


def _conv_module(ext_ref, sh_ref, first_row, row0, rows, cw_ref, cb_ref, lg_ref, lb_ref):
    y = cb_ref[...]
    for j in range(CONV_WIDTH):
        base, s = (first_row + j) // SUBLANES * SUBLANES, (first_row + j) % SUBLANES
        at = pl.ds(base + row0, rows)
        y = y + cw_ref[j:j + 1, :] * (ext_ref[at, :] if s == 0 else sh_ref[s - 1, at, :])
    yc = y - jnp.mean(y, axis=-1, keepdims=True)
    yn = yc * lax.rsqrt(jnp.mean(yc * yc, axis=-1, keepdims=True) + EPS) * lg_ref[...] + lb_ref[...]
    return yn * jax.nn.sigmoid(yn)


def _softmax_rows(s):
    m = jnp.max(s, axis=-1, keepdims=True)
    p = jnp.exp(s - m)
    return p.astype(BF16), 1.0 / jnp.sum(p, axis=-1, keepdims=True)


def _mix_prompt_kernel(x_ref, u_ref, halo_ref, oa_ref, mk_ref, mv_ref, cw_ref, cb_ref, lg_ref, lb_ref,
                       wout_ref, gx_ref, wq_ref, wo_ref, y_ref, ext_ref, sh_ref):
    ts = x_ref.shape[0]
    halo = halo_ref[...]
    ext_ref[0:HALO, :] = jnp.where(pl.program_id(1) == 0, jnp.zeros_like(halo), halo)
    ext_ref[HALO:HALO + ts, :] = u_ref[...]
    _realign_conv_input(ext_ref, sh_ref)
    ob = _conv_module(ext_ref, sh_ref, HALO - (CONV_WIDTH - 1), 0, ts, cw_ref, cb_ref, lg_ref, lb_ref)
    oa = jnp.concatenate([oa_ref[p] for p in range(oa_ref.shape[0])], axis=-1)
    x2 = (x_ref[...] + _mm(oa.astype(BF16), wout_ref[0:WIDTH_A, :])
          + _mm(ob.astype(BF16), wout_ref[WIDTH_A:WIDTH_A + WIDTH_B, :]))
    hx = _rms(x2, gx_ref[...]).astype(BF16)
    qx = (_mm(hx, wq_ref[...]) * (HEAD_DIM_X ** -0.5)).astype(BF16)
    outs = []
    for h in range(N_HEADS_X):
        sl = slice(h * HEAD_DIM_X, (h + 1) * HEAD_DIM_X)
        p, inv = _softmax_rows(_mm_nt(qx[:, sl], mk_ref[:, sl]))
        outs.append((_mm(p, mv_ref[:, sl]) * inv).astype(BF16))
    y_ref[...] = x2 + _mm(jnp.concatenate(outs, axis=-1), wo_ref[...])


def _mix_prompt(x, u, oa, mk, mv, cw, cb, lg, lb, wout, gx, wq, wo, batch, seq, ts):
    tiles = seq // ts
    row = lambda w: pl.BlockSpec((ts, w), lambda b, i: (b * tiles + i, 0))
    oa_spec = pl.BlockSpec((WIDTH_A // LANES, ts, LANES), lambda b, i: (0, b * tiles + i, 0))
    halo = pl.BlockSpec((HALO, WIDTH_B),
                        lambda b, i: (jnp.maximum((b * seq + i * ts) // HALO - 1, 0), 0))
    mem = pl.BlockSpec((N_MEM, D_MODEL), lambda b, i: (b, 0))
    vec = lambda w: _const_spec((1, w))
    return pl.pallas_call(
        _mix_prompt_kernel,
        grid=(batch, tiles),
        in_specs=[row(D_MODEL), row(WIDTH_B), halo, oa_spec, mem, mem, _const_spec(cw.shape),
                  vec(WIDTH_B), vec(WIDTH_B), vec(WIDTH_B), _const_spec(wout.shape), vec(D_MODEL),
                  _const_spec(wq.shape), _const_spec(wo.shape)],
        out_specs=row(D_MODEL),
        out_shape=jax.ShapeDtypeStruct((batch * seq, D_MODEL), F32),
        scratch_shapes=[pltpu.VMEM((HALO + ts, WIDTH_B), F32),
                        pltpu.VMEM((SUBLANES - 1, _shifted_rows(HALO - (CONV_WIDTH - 1), ts), WIDTH_B), F32)],
        compiler_params=_params("parallel", "parallel"),
        name="mix_prompt",
    )(x, u, u, oa, mk, mv, cw, cb, lg, lb, wout, gx, wq, wo)


def _mix_sample_kernel(x_ref, uext_ref, oa_ref, mk_ref, mv_ref, cw_ref, cb_ref, lg_ref, lb_ref,
                       wout_ref, gx_ref, wq_ref, wo_ref, y_ref, sh_ref, *, group, t_new, t_ext):
    rows = group * t_ext
    _realign_conv_input(uext_ref, sh_ref)
    conv = _conv_module(uext_ref, sh_ref, 0, 0, rows, cw_ref, cb_ref, lg_ref, lb_ref)
    ob = conv.reshape(group, t_ext, WIDTH_B)[:, 0:t_new, :].reshape(group * t_new, WIDTH_B)
    x2 = (x_ref[...] + _mm(oa_ref[...].astype(BF16), wout_ref[0:WIDTH_A, :])
          + _mm(ob.astype(BF16), wout_ref[WIDTH_A:WIDTH_A + WIDTH_B, :]))

    hx = _rms(x2, gx_ref[...]).astype(BF16)
    qx = _mm(hx, wq_ref[...]) * (HEAD_DIM_X ** -0.5)
    qrows_n = N_HEADS_X * t_new
    rr = lax.broadcasted_iota(jnp.int32, (qrows_n, D_MODEL), 0)
    ll = lax.broadcasted_iota(jnp.int32, (qrows_n, D_MODEL), 1)
    own = rr // t_new == ll // HEAD_DIM_X
    lane_head = lax.broadcasted_iota(jnp.int32, (t_new, D_MODEL), 1) // HEAD_DIM_X
    outs = []
    for b in range(group):
        qb = qx[b * t_new:(b + 1) * t_new, :]
        qrows = jnp.where(own, jnp.concatenate([qb] * N_HEADS_X, axis=0), 0.0).astype(BF16)
        mk = mk_ref[b].reshape(N_MEM, D_MODEL).astype(BF16)
        mv = mv_ref[b].reshape(N_MEM, D_MODEL).astype(BF16)
        p, inv = _softmax_rows(_mm_nt(qrows, mk))
        o = _mm(p, mv) * inv
        out = jnp.zeros((t_new, D_MODEL), F32)
        for h in range(N_HEADS_X):
            out = jnp.where(lane_head == h, o[h * t_new:(h + 1) * t_new, :], out)
        outs.append(out)
    y_ref[...] = x2 + _mm(jnp.concatenate(outs, axis=0).astype(BF16), wo_ref[...])


def _mix_sample(x, uext, oa, mk, mv, cw, cb, lg, lb, wout, gx, wq, wo, batch, t_new, t_ext, group):
    rows = group * t_new
    row = lambda w: pl.BlockSpec((rows, w), lambda i: (i, 0))
    ext = pl.BlockSpec((group * t_ext + HALO, WIDTH_B), lambda i: (i, 0))
    mem = pl.BlockSpec((group, N_MEM, N_HEADS_X, HEAD_DIM_X), lambda i: (i, 0, 0, 0))
    vec = lambda w: _const_spec((1, w))
    return pl.pallas_call(
        functools.partial(_mix_sample_kernel, group=group, t_new=t_new, t_ext=t_ext),
        grid=(batch // group,),
        in_specs=[row(D_MODEL), ext, row(WIDTH_A), mem, mem, _const_spec(cw.shape),
                  vec(WIDTH_B), vec(WIDTH_B), vec(WIDTH_B), _const_spec(wout.shape), vec(D_MODEL),
                  _const_spec(wq.shape), _const_spec(wo.shape)],
        out_specs=row(D_MODEL),
        out_shape=jax.ShapeDtypeStruct((batch * t_new, D_MODEL), F32),
        scratch_shapes=[pltpu.VMEM((SUBLANES - 1, _shifted_rows(0, group * t_ext), WIDTH_B), F32)],
        compiler_params=_params("parallel"),
        name="mix_sample",
    )(x, uext, oa, mk, mv, cw, cb, lg, lb, wout, gx, wq, wo)


def kernel(x_prompt, x_sample, mem_prompt, cache_win_k, cache_win_v, cache_conv, cache_mem_k, cache_mem_v, ffn1_norm, ffn1_gate, ffn1_up, ffn1_down, mix_norm, w_in, conv_w, conv_b, conv_ln_g, conv_ln_b, w_out, xattn_norm, mem_norm, w_cq, w_ck, w_cv, w_co, ffn2_norm, ffn2_gate, ffn2_up, ffn2_down, final_norm):
    depth = ffn1_norm.shape[0]
    assert depth == 1
    bp, seq, _ = x_prompt.shape
    bs, t_new, _ = x_sample.shape
    keep = CONV_WIDTH - 1
    l = 0
    vec = lambda a: a.reshape(1, -1)
    bf = lambda a: a.astype(BF16)
    slopes = jnp.asarray([2.0 ** -(h + 1) for h in range(N_HEADS_A)], F32)

    f1 = (vec(ffn1_norm[l]), bf(ffn1_gate[l]), bf(ffn1_up[l]), bf(ffn1_down[l]))
    f2 = (vec(ffn2_norm[l]), bf(ffn2_gate[l]), bf(ffn2_up[l]), bf(ffn2_down[l]))
    gmix, win = vec(mix_norm[l]), bf(w_in[l])
    conv = (conv_w[l], vec(conv_b[l]), vec(conv_ln_g[l]), vec(conv_ln_b[l]))
    proj = (bf(w_out[l]), vec(xattn_norm[l]), bf(w_cq[l]), bf(w_co[l]))
    gfin = vec(final_norm)

    xp = x_prompt.reshape(bp * seq, D_MODEL)
    x1, q, k, v, u, kt, vt = _ffn_proj(xp, *f1, gmix, win, tm=512, seq=seq)
    oa = _attn_prompt(q, k, v, slopes, bp, seq)
    mk, mv, mkh, mvh = _mem_kv(mem_prompt.reshape(bp * N_MEM, D_MODEL), vec(mem_norm[l]), bf(w_ck[l]), bf(w_cv[l]),
                               tm=512)
    x3 = _mix_prompt(x1, u, oa, mk, mv, *conv, *proj, batch=bp, seq=seq, ts=512)
    yp = _ffn_final(x3, *f2, gfin, tm=1024)

    xs = x_sample.reshape(bs * t_new, D_MODEL)
    s1, sq, sk, sv, su, skh, svh = _ffn_proj(xs, *f1, gmix, win, tm=bs * t_new)
    seq_minor = lambda c: jnp.transpose(c, (0, 2, 3, 1)).reshape(c.shape[0], WIDTH_A, c.shape[1])
    soa = _attn_sample(sq, sk, sv, seq_minor(cache_win_k[l]), seq_minor(cache_win_v[l]), t_new)
    u_ext = jnp.concatenate([cache_conv[l], su.reshape(bs, t_new, WIDTH_B)], axis=1)
    t_ext = -(-(keep + t_new) // 8) * 8
    group = 8
    u_pad = jnp.pad(u_ext, ((0, 0), (0, t_ext - keep - t_new), (0, 0))).reshape(bs // group, group * t_ext, WIDTH_B)
    u_pad = jnp.pad(u_pad, ((0, 0), (0, HALO), (0, 0))).reshape(-1, WIDTH_B)
    s3 = _mix_sample(s1, u_pad, soa, cache_mem_k[l], cache_mem_v[l], *conv, *proj,
                     batch=bs, t_new=t_new, t_ext=t_ext, group=group)
    ys = _ffn_final(s3, *f2, gfin, tm=bs * t_new)

    heads = lambda a, b, t: a.reshape(1, b, t, N_HEADS_A, HEAD_DIM)
    from_seq_minor = lambda a: jnp.transpose(a.reshape(1, bp, N_HEADS_A, HEAD_DIM, seq), (0, 1, 4, 2, 3))
    mem_heads = lambda a: a.reshape(1, bp, N_MEM, N_HEADS_X, HEAD_DIM_X)
    return (yp.reshape(bp, seq, D_MODEL),
            ys.reshape(bs, t_new, D_MODEL),
            from_seq_minor(kt), from_seq_minor(vt),
            u.reshape(1, bp, seq, WIDTH_B)[:, :, seq - keep:],
            mem_heads(mkh), mem_heads(mvh),
            heads(skh, bs, t_new), heads(svh, bs, t_new),
            u_ext[None, :, t_new:])
```

```python
import functools

import jax
import jax.numpy as jnp
from jax import lax
from jax.experimental import pallas as pl
from jax.experimental.pallas import tpu as pltpu

D_MODEL = 1024
HEAD_DIM = 64
N_HEADS_A = 8
WIDTH_A = N_HEADS_A * HEAD_DIM
WIDTH_B = D_MODEL - WIDTH_A
DILATED_BRANCHES = ((128, 1), (512, 4), (2048, 16))
BAND = 128
CONV_WIDTH = 31
D_FF = 2816
N_MEM = 256
N_HEADS_X = 4
HEAD_DIM_X = D_MODEL // N_HEADS_X
EPS = 1e-6

LANES = 128
SUBLANES = 8
FF_CHUNK = 256
HALO = 32
ATTN_GROUP = 2
NEG_BIG = -1e30
VMEM_LIMIT = 56 * 1024 * 1024

F32 = jnp.float32
BF16 = jnp.bfloat16


def _const_spec(shape):
    nd = len(shape)
    return pl.BlockSpec(shape, lambda *_: (0,) * nd, pipeline_mode=pl.Buffered(1))


def _params(*sem):
    return pltpu.CompilerParams(dimension_semantics=sem, vmem_limit_bytes=VMEM_LIMIT)


def _rms(x, g):
    return x * lax.rsqrt(jnp.mean(x * x, axis=-1, keepdims=True) + EPS) * g


def _mm(a, b):
    return jnp.dot(a, b, preferred_element_type=F32)


def _mm_nt(a, b):
    return lax.dot_general(a, b, (((1,), (1,)), ((), ())), preferred_element_type=F32)


def _swiglu_half_step(x, g_ref, wg_ref, wu_ref, wd_ref, act_ref):
    h = _rms(x, g_ref[...]).astype(BF16)
    for c in range(0, D_FF, FF_CHUNK):
        gate = _mm(h, wg_ref[:, c:c + FF_CHUNK])
        up = _mm(h, wu_ref[:, c:c + FF_CHUNK])
        act_ref[:, c:c + FF_CHUNK] = (gate * jax.nn.sigmoid(gate) * up).astype(BF16)
    return x + 0.5 * _mm(act_ref[...], wd_ref[...])


def _ffn_proj_kernel(x_ref, g_ref, wg_ref, wu_ref, wd_ref, gmix_ref, win_ref,
                     x1_ref, q_ref, k_ref, v_ref, u_ref, kh_ref, vh_ref, act_ref):
    def put_cols(ref, val):
        if len(ref.shape) == 2:
            ref[...] = val
        else:
            for p in range(ref.shape[0]):
                ref[p] = val[:, p * LANES:(p + 1) * LANES]

    x1 = _swiglu_half_step(x_ref[...], g_ref, wg_ref, wu_ref, wd_ref, act_ref)
    x1_ref[...] = x1
    h = _rms(x1, gmix_ref[...]).astype(BF16)
    w = WIDTH_A
    put_cols(q_ref, _mm(h, win_ref[:, 0:w]) * (HEAD_DIM ** -0.5))
    k = _mm(h, win_ref[:, w:2 * w])
    v = _mm(h, win_ref[:, 2 * w:3 * w])
    put_cols(k_ref, k)
    put_cols(v_ref, v)
    if len(kh_ref.shape) == 2:
        kh_ref[...] = k.T
        vh_ref[...] = v.T
    else:
        kh_ref[...] = k.reshape(kh_ref.shape)
        vh_ref[...] = v.reshape(vh_ref.shape)
    a = _mm(h, win_ref[:, 3 * w:3 * w + WIDTH_B])
    g = _mm(h, win_ref[:, 3 * w + WIDTH_B:3 * w + 2 * WIDTH_B])
    u_ref[...] = a * jax.nn.sigmoid(g)


def _ffn_proj(x, g, wg, wu, wd, gmix, win, tm, seq=None):
    n = x.shape[0]
    row = lambda w: pl.BlockSpec((tm, w), lambda i: (i, 0))
    if seq is None:
        heads = pl.BlockSpec((tm, N_HEADS_A, HEAD_DIM), lambda i: (i, 0, 0))
        heads_shape = jax.ShapeDtypeStruct((n, N_HEADS_A, HEAD_DIM), F32)
        qkv = row(WIDTH_A)
        qkv_shape = jax.ShapeDtypeStruct((n, WIDTH_A), F32)
    else:
        tiles = seq // tm
        heads = pl.BlockSpec((None, WIDTH_A, tm), lambda i: (i // tiles, 0, i % tiles))
        heads_shape = jax.ShapeDtypeStruct((n // seq, WIDTH_A, seq), F32)
        qkv = pl.BlockSpec((WIDTH_A // LANES, tm, LANES), lambda i: (0, i, 0))
        qkv_shape = jax.ShapeDtypeStruct((WIDTH_A // LANES, n, LANES), F32)
    return pl.pallas_call(
        _ffn_proj_kernel,
        grid=(n // tm,),
        in_specs=[row(D_MODEL), _const_spec((1, D_MODEL)), _const_spec(wg.shape), _const_spec(wu.shape),
                  _const_spec(wd.shape), _const_spec((1, D_MODEL)), _const_spec(win.shape)],
        out_specs=[row(D_MODEL), qkv, qkv, qkv, row(WIDTH_B), heads, heads],
        out_shape=[jax.ShapeDtypeStruct((n, D_MODEL), F32)] + [qkv_shape] * 3
        + [jax.ShapeDtypeStruct((n, WIDTH_B), F32)] + [heads_shape] * 2,
        scratch_shapes=[pltpu.VMEM((tm, D_FF), BF16)],
        compiler_params=_params("parallel"),
        name="ffn_proj",
    )(x, g, wg, wu, wd, gmix, win)


def _ffn_final_kernel(x_ref, g_ref, wg_ref, wu_ref, wd_ref, gfin_ref, y_ref, act_ref):
    x1 = _swiglu_half_step(x_ref[...], g_ref, wg_ref, wu_ref, wd_ref, act_ref)
    y_ref[...] = _rms(x1, gfin_ref[...])


def _ffn_final(x, g, wg, wu, wd, gfin, tm):
    n = x.shape[0]
    row = pl.BlockSpec((tm, D_MODEL), lambda i: (i, 0))
    return pl.pallas_call(
        _ffn_final_kernel,
        grid=(n // tm,),
        in_specs=[row, _const_spec((1, D_MODEL)), _const_spec(wg.shape), _const_spec(wu.shape),
                  _const_spec(wd.shape), _const_spec((1, D_MODEL))],
        out_specs=row,
        out_shape=jax.ShapeDtypeStruct((n, D_MODEL), F32),
        scratch_shapes=[pltpu.VMEM((tm, D_FF), BF16)],
        compiler_params=_params("parallel"),
        name="ffn_final",
    )(x, g, wg, wu, wd, gfin)


def _attn_prompt_kernel(slopes_ref, q_ref, k_ref, v_ref, o_ref,
                        x4_ref, qs_ref, kp_ref, vp_ref, bias_ref, s_ref, p_ref, m_ref, l_ref, n_ref,
                        fm_ref, fl_ref, fn_ref, *, seq):
    pair = pl.program_id(0)
    nb = seq // BAND
    lo = lax.broadcasted_iota(jnp.int32, (BAND, LANES), 1) < HEAD_DIM

    @pl.when(pl.program_id(1) == 0)
    def _():
        kp_ref[:, 0:BAND, :] = jnp.zeros((3, BAND, LANES), BF16)
        vp_ref[:, 0:BAND, 0:LANES] = jnp.zeros((3, BAND, LANES), BF16)
        vp_ref[:, :, LANES:2 * LANES] = jnp.ones((3, BAND + seq, LANES), BF16)
        qi = lax.broadcasted_iota(jnp.int32, (BAND, 2 * BAND), 0)
        kj = lax.broadcasted_iota(jnp.int32, (BAND, 2 * BAND), 1)
        dist = qi + BAND - kj
        neg_dist = jnp.where((dist >= 0) & (dist <= BAND), -dist.astype(F32), NEG_BIG)
        neg_dist_cur = jnp.where(kj >= BAND, neg_dist, NEG_BIG)
        for b, (_, dil) in enumerate(DILATED_BRANCHES):
            for noprev in range(2):
                if 2 * b + noprev < bias_ref.shape[0]:
                    table = neg_dist_cur if noprev else neg_dist
                    for half in range(2):
                        bias_ref[2 * b + noprev, half * BAND:(half + 1) * BAND, :] = (
                            (table * slopes_ref[2 * pair + half]) * float(dil))

    def put(kind, b, first, x):
        nblocks = x.shape[0] // BAND
        if kind == 0:
            is_lo = lax.broadcasted_iota(jnp.int32, x.shape, 1) < HEAD_DIM
            x_lo = jnp.where(is_lo, x, 0.0).astype(BF16)
            x_hi = jnp.where(is_lo, 0.0, x).astype(BF16)
            for j in range(nblocks):
                qs_ref[b, first + j, 0] = x_lo[j * BAND:(j + 1) * BAND]
                qs_ref[b, first + j, 1] = x_hi[j * BAND:(j + 1) * BAND]
        elif kind == 1:
            kp_ref[b, BAND + first * BAND:BAND + first * BAND + x.shape[0], :] = x.astype(BF16)
        else:
            vp_ref[b, BAND + first * BAND:BAND + first * BAND + x.shape[0], 0:LANES] = x.astype(BF16)

    quarter = seq // 4
    for kind, ref in enumerate((q_ref, k_ref, v_ref)):
        for r in range(4):
            put(kind, 0, 4 * r, ref[r * quarter:(r + 1) * quarter, :])
            x = ref[pl.ds(r, quarter, stride=4), :]
            x4_ref[r * quarter:(r + 1) * quarter, :] = x
            put(kind, 1, 4 * r, x)
        for r in range(16):
            put(kind, 2, r, x4_ref[pl.ds((r % 4) * quarter + r // 4, seq // 16, stride=4), :])

    def blocks_per_class(b):
        return seq // DILATED_BRANCHES[b][1] // BAND

    def keys_of(b):
        return 2 * BAND if blocks_per_class(b) > 1 else BAND

    def scores(n):
        for b in range(3):
            qb = qs_ref[b, n].reshape(2 * BAND, LANES)
            if blocks_per_class(b) > 1:
                kb = kp_ref[b, n * BAND:(n + 2) * BAND, :]
                bias = bias_ref[2 * b + int(n % blocks_per_class(b) == 0)]
            else:
                kb = kp_ref[b, (n + 1) * BAND:(n + 2) * BAND, :]
                bias = bias_ref[2 * b, :, BAND:2 * BAND]
            s_ref[n, b, :, 0:keys_of(b)] = _mm_nt(qb, kb) + bias

    def softmax(n):
        for b in range(3):
            s = s_ref[n, b, :, 0:keys_of(b)]
            m = jnp.max(s, axis=-1, keepdims=True)
            p_ref[n, b, :, 0:keys_of(b)] = jnp.exp(s - m).astype(BF16)
            m_ref[b, n * BAND:(n + 1) * BAND, :] = jnp.where(lo, m[0:BAND], m[BAND:2 * BAND])

    def weighted_values(n):
        for b in range(3):
            first = n * BAND + (2 * BAND - keys_of(b))
            pv = _mm(p_ref[n, b, :, 0:keys_of(b)], vp_ref[b, first:first + keys_of(b), :])
            rows = slice(n * BAND, (n + 1) * BAND)
            n_ref[b, rows, :] = jnp.where(lo, pv[0:BAND, 0:LANES], pv[BAND:2 * BAND, 0:LANES])
            l_ref[b, rows, :] = jnp.where(lo, pv[0:BAND, LANES:2 * LANES], pv[BAND:2 * BAND, LANES:2 * LANES])

    groups = nb // ATTN_GROUP
    group = lambda i: range(i * ATTN_GROUP, (i + 1) * ATTN_GROUP) if 0 <= i < groups else ()
    for n in group(0):
        scores(n)
    for i in range(groups + 1):
        for n in group(i - 1):
            weighted_values(n)
        for n in group(i + 1):
            scores(n)
        for n in group(i):
            softmax(n)

    per16 = seq // 16
    for r in range(16):
        sl4 = pl.ds((r % 4) * quarter + r // 4, per16, stride=4)
        blk = pl.ds(r * per16, per16)
        m1, m2 = m_ref[1, sl4, :], m_ref[2, blk, :]
        mx = jnp.maximum(m1, m2)
        e1, e2 = jnp.exp(m1 - mx), jnp.exp(m2 - mx)
        fm_ref[sl4, :] = mx
        fl_ref[sl4, :] = e1 * l_ref[1, sl4, :] + e2 * l_ref[2, blk, :]
        fn_ref[sl4, :] = e1 * n_ref[1, sl4, :] + e2 * n_ref[2, blk, :]
    for r in range(4):
        for c in range(quarter // BAND):
            sl = pl.ds(r + 4 * BAND * c, BAND, stride=4)
            blk = pl.ds(r * quarter + c * BAND, BAND)
            m0, m1 = m_ref[0, sl, :], fm_ref[blk, :]
            mx = jnp.maximum(m0, m1)
            e0, e1 = jnp.exp(m0 - mx), jnp.exp(m1 - mx)
            den = e0 * l_ref[0, sl, :] + e1 * fl_ref[blk, :]
            num = e0 * n_ref[0, sl, :] + e1 * fn_ref[blk, :]
            o_ref[sl, :] = num / den


def _attn_prompt(q, k, v, slopes, batch, seq):
    assert [seq // d // BAND for _, d in DILATED_BRANCHES] == [16, 4, 1]
    nb = seq // BAND
    blk = pl.BlockSpec((None, seq, LANES), lambda p, b: (p, b, 0))
    return pl.pallas_call(
        functools.partial(_attn_prompt_kernel, seq=seq),
        grid=(WIDTH_A // LANES, batch),
        in_specs=[pl.BlockSpec(memory_space=pltpu.SMEM), blk, blk, blk],
        out_specs=blk,
        out_shape=jax.ShapeDtypeStruct((WIDTH_A // LANES, batch * seq, LANES), F32),
        scratch_shapes=[pltpu.VMEM((seq, LANES), F32),
                        pltpu.VMEM((3, nb, 2, BAND, LANES), BF16),
                        pltpu.VMEM((3, BAND + seq, LANES), BF16),
                        pltpu.VMEM((3, BAND + seq, 2 * LANES), BF16),
                        pltpu.VMEM((5, 2 * BAND, 2 * BAND), F32),
                        pltpu.VMEM((nb, 3, 2 * BAND, 2 * BAND), F32),
                        pltpu.VMEM((nb, 3, 2 * BAND, 2 * BAND), BF16)]
        + [pltpu.VMEM((3, seq, LANES), F32)] * 3 + [pltpu.VMEM((seq, LANES), F32)] * 3,
        compiler_params=_params("parallel", "arbitrary"),
        name="attn_prompt",
    )(slopes, q, k, v)


def _attn_sample_kernel(q_ref, kn_ref, vn_ref, kt_ref, vt_ref, o_ref, *, t_new, w_buf, pad):
    rows = N_HEADS_A * t_new
    zeros = jnp.zeros((pad - t_new, WIDTH_A), F32)
    k_new = jnp.concatenate([kn_ref[...], zeros], axis=0).astype(BF16)
    v_new = jnp.concatenate([vn_ref[...], zeros], axis=0).astype(BF16)

    qrep = jnp.concatenate([q_ref[...]] * N_HEADS_A, axis=0)
    rr = lax.broadcasted_iota(jnp.int32, (rows, WIDTH_A), 0)
    ll = lax.broadcasted_iota(jnp.int32, (rows, WIDTH_A), 1)
    qrows = jnp.where(rr // t_new == ll // HEAD_DIM, qrep, 0.0).astype(BF16)
    s = jnp.concatenate([_mm(qrows, kt_ref[...].astype(BF16)), _mm_nt(qrows, k_new)], axis=1)

    r2 = lax.broadcasted_iota(jnp.int32, s.shape, 0)
    u2 = lax.broadcasted_iota(jnp.int32, s.shape, 1)
    d = w_buf + r2 % t_new - u2
    mult = jnp.zeros(s.shape, F32)
    for window, dil in DILATED_BRANCHES:
        mult = mult + jnp.where((d >= 0) & (d <= window) & (d % dil == 0), 1.0, 0.0)
    head = lax.broadcasted_iota(jnp.int32, (rows, 1), 0) // t_new
    slope = jnp.zeros((rows, 1), F32)
    for h in range(N_HEADS_A):
        slope = jnp.where(head == h, 2.0 ** -(h + 1), slope)
    s = jnp.where(mult > 0, s - d.astype(F32) * slope, NEG_BIG)
    m = jnp.max(s, axis=-1, keepdims=True)
    p = mult * jnp.exp(s - m)
    den = jnp.sum(p, axis=-1, keepdims=True)
    p = p.astype(BF16)
    o = (_mm_nt(p[:, 0:w_buf], vt_ref[...].astype(BF16)) + _mm(p[:, w_buf:w_buf + pad], v_new)) * (1.0 / den)

    lane_head = lax.broadcasted_iota(jnp.int32, (t_new, WIDTH_A), 1) // HEAD_DIM
    out = jnp.zeros((t_new, WIDTH_A), F32)
    for h in range(N_HEADS_A):
        out = jnp.where(lane_head == h, o[h * t_new:(h + 1) * t_new, :], out)
    o_ref[...] = out


def _attn_sample(q, k, v, cache_kt, cache_vt, t_new):
    batch, _, w_buf = cache_kt.shape
    assert t_new % SUBLANES == 0 and w_buf >= DILATED_BRANCHES[-1][0] and w_buf % LANES == 0
    pad = LANES
    new = pl.BlockSpec((t_new, WIDTH_A), lambda b: (b, 0))
    cache = pl.BlockSpec((None, WIDTH_A, w_buf), lambda b: (b, 0, 0))
    return pl.pallas_call(
        functools.partial(_attn_sample_kernel, t_new=t_new, w_buf=w_buf, pad=pad),
        grid=(batch,),
        in_specs=[new, new, new, cache, cache],
        out_specs=new,
        out_shape=jax.ShapeDtypeStruct((batch * t_new, WIDTH_A), F32),
        compiler_params=_params("parallel"),
        name="attn_sample",
    )(q, k, v, cache_kt, cache_vt)


def _mem_kv_kernel(mem_ref, g_ref, wk_ref, wv_ref, mk_ref, mv_ref, mkh_ref, mvh_ref):
    h = _rms(mem_ref[...], g_ref[...]).astype(BF16)
    mk = _mm(h, wk_ref[...])
    mv = _mm(h, wv_ref[...])
    mk_ref[...] = mk.astype(BF16)
    mv_ref[...] = mv.astype(BF16)
    mkh_ref[...] = mk.reshape(mkh_ref.shape)
    mvh_ref[...] = mv.reshape(mvh_ref.shape)


def _mem_kv(mem, g, wk, wv, tm):
    n = mem.shape[0]
    row = pl.BlockSpec((tm, D_MODEL), lambda i: (i, 0))
    heads = pl.BlockSpec((tm, N_HEADS_X, HEAD_DIM_X), lambda i: (i, 0, 0))
    return pl.pallas_call(
        _mem_kv_kernel,
        grid=(n // tm,),
        in_specs=[row, _const_spec((1, D_MODEL)), _const_spec(wk.shape), _const_spec(wv.shape)],
        out_specs=[row, row, heads, heads],
        out_shape=[jax.ShapeDtypeStruct((n, D_MODEL), BF16)] * 2
        + [jax.ShapeDtypeStruct((n, N_HEADS_X, HEAD_DIM_X), F32)] * 2,
        compiler_params=_params("parallel"),
        name="mem_kv",
    )(mem, g, wk, wv)


def _shifted_rows(first_row, rows):
    offs = [first_row + j for j in range(CONV_WIDTH) if (first_row + j) % SUBLANES]
    return max(offs) // SUBLANES * SUBLANES + rows


def _realign_conv_input(ext_ref, sh_ref):
    for s in range(1, SUBLANES):
        sh_ref[s - 1] = ext_ref[pl.ds(s, sh_ref.shape[1]), :]


def _conv_module(ext_ref, sh_ref, first_row, row0, rows, cw_ref, cb_ref, lg_ref, lb_ref):
    y = cb_ref[...]
    for j in range(CONV_WIDTH):
        base, s = (first_row + j) // SUBLANES * SUBLANES, (first_row + j) % SUBLANES
        at = pl.ds(base + row0, rows)
        y = y + cw_ref[j:j + 1, :] * (ext_ref[at, :] if s == 0 else sh_ref[s - 1, at, :])
    yc = y - jnp.mean(y, axis=-1, keepdims=True)
    yn = yc * lax.rsqrt(jnp.mean(yc * yc, axis=-1, keepdims=True) + EPS) * lg_ref[...] + lb_ref[...]
    return yn * jax.nn.sigmoid(yn)


def _softmax_rows(s):
    m = jnp.max(s, axis=-1, keepdims=True)
    p = jnp.exp(s - m)
    return p.astype(BF16), 1.0 / jnp.sum(p, axis=-1, keepdims=True)


def _mix_prompt_kernel(x_ref, u_ref, halo_ref, oa_ref, mk_ref, mv_ref, cw_ref, cb_ref, lg_ref, lb_ref,
                       wout_ref, gx_ref, wq_ref, wo_ref, y_ref, ext_ref, sh_ref):
    ts = x_ref.shape[0]
    halo = halo_ref[...]
    ext_ref[0:HALO, :] = jnp.where(pl.program_id(1) == 0, jnp.zeros_like(halo), halo)
    ext_ref[HALO:HALO + ts, :] = u_ref[...]
    _realign_conv_input(ext_ref, sh_ref)
    ob = _conv_module(ext_ref, sh_ref, HALO - (CONV_WIDTH - 1), 0, ts, cw_ref, cb_ref, lg_ref, lb_ref)
    oa = jnp.concatenate([oa_ref[p] for p in range(oa_ref.shape[0])], axis=-1)
    x2 = (x_ref[...] + _mm(oa.astype(BF16), wout_ref[0:WIDTH_A, :])
          + _mm(ob.astype(BF16), wout_ref[WIDTH_A:WIDTH_A + WIDTH_B, :]))
    hx = _rms(x2, gx_ref[...]).astype(BF16)
    qx = (_mm(hx, wq_ref[...]) * (HEAD_DIM_X ** -0.5)).astype(BF16)
    outs = []
    for h in range(N_HEADS_X):
        sl = slice(h * HEAD_DIM_X, (h + 1) * HEAD_DIM_X)
        p, inv = _softmax_rows(_mm_nt(qx[:, sl], mk_ref[:, sl]))
        outs.append((_mm(p, mv_ref[:, sl]) * inv).astype(BF16))
    y_ref[...] = x2 + _mm(jnp.concatenate(outs, axis=-1), wo_ref[...])


def _mix_prompt(x, u, oa, mk, mv, cw, cb, lg, lb, wout, gx, wq, wo, batch, seq, ts):
    tiles = seq // ts
    row = lambda w: pl.BlockSpec((ts, w), lambda b, i: (b * tiles + i, 0))
    oa_spec = pl.BlockSpec((WIDTH_A // LANES, ts, LANES), lambda b, i: (0, b * tiles + i, 0))
    halo = pl.BlockSpec((HALO, WIDTH_B),
                        lambda b, i: (jnp.maximum((b * seq + i * ts) // HALO - 1, 0), 0))
    mem = pl.BlockSpec((N_MEM, D_MODEL), lambda b, i: (b, 0))
    vec = lambda w: _const_spec((1, w))
    return pl.pallas_call(
        _mix_prompt_kernel,
        grid=(batch, tiles),
        in_specs=[row(D_MODEL), row(WIDTH_B), halo, oa_spec, mem, mem, _const_spec(cw.shape),
                  vec(WIDTH_B), vec(WIDTH_B), vec(WIDTH_B), _const_spec(wout.shape), vec(D_MODEL),
                  _const_spec(wq.shape), _const_spec(wo.shape)],
        out_specs=row(D_MODEL),
        out_shape=jax.ShapeDtypeStruct((batch * seq, D_MODEL), F32),
        scratch_shapes=[pltpu.VMEM((HALO + ts, WIDTH_B), F32),
                        pltpu.VMEM((SUBLANES - 1, _shifted_rows(HALO - (CONV_WIDTH - 1), ts), WIDTH_B), F32)],
        compiler_params=_params("parallel", "parallel"),
        name="mix_prompt",
    )(x, u, u, oa, mk, mv, cw, cb, lg, lb, wout, gx, wq, wo)


def _mix_sample_kernel(x_ref, uext_ref, oa_ref, mk_ref, mv_ref, cw_ref, cb_ref, lg_ref, lb_ref,
                       wout_ref, gx_ref, wq_ref, wo_ref, y_ref, sh_ref, *, group, t_new, t_ext):
    rows = group * t_ext
    _realign_conv_input(uext_ref, sh_ref)
    conv = _conv_module(uext_ref, sh_ref, 0, 0, rows, cw_ref, cb_ref, lg_ref, lb_ref)
    ob = conv.reshape(group, t_ext, WIDTH_B)[:, 0:t_new, :].reshape(group * t_new, WIDTH_B)
    x2 = (x_ref[...] + _mm(oa_ref[...].astype(BF16), wout_ref[0:WIDTH_A, :])
          + _mm(ob.astype(BF16), wout_ref[WIDTH_A:WIDTH_A + WIDTH_B, :]))

    hx = _rms(x2, gx_ref[...]).astype(BF16)
    qx = _mm(hx, wq_ref[...]) * (HEAD_DIM_X ** -0.5)
    qrows_n = N_HEADS_X * t_new
    rr = lax.broadcasted_iota(jnp.int32, (qrows_n, D_MODEL), 0)
    ll = lax.broadcasted_iota(jnp.int32, (qrows_n, D_MODEL), 1)
    own = rr // t_new == ll // HEAD_DIM_X
    lane_head = lax.broadcasted_iota(jnp.int32, (t_new, D_MODEL), 1) // HEAD_DIM_X
    outs = []
    for b in range(group):
        qb = qx[b * t_new:(b + 1) * t_new, :]
        qrows = jnp.where(own, jnp.concatenate([qb] * N_HEADS_X, axis=0), 0.0).astype(BF16)
        mk = mk_ref[b].reshape(N_MEM, D_MODEL).astype(BF16)
        mv = mv_ref[b].reshape(N_MEM, D_MODEL).astype(BF16)
        p, inv = _softmax_rows(_mm_nt(qrows, mk))
        o = _mm(p, mv) * inv
        out = jnp.zeros((t_new, D_MODEL), F32)
        for h in range(N_HEADS_X):
            out = jnp.where(lane_head == h, o[h * t_new:(h + 1) * t_new, :], out)
        outs.append(out)
    y_ref[...] = x2 + _mm(jnp.concatenate(outs, axis=0).astype(BF16), wo_ref[...])


def _mix_sample(x, uext, oa, mk, mv, cw, cb, lg, lb, wout, gx, wq, wo, batch, t_new, t_ext, group):
    rows = group * t_new
    row = lambda w: pl.BlockSpec((rows, w), lambda i: (i, 0))
    ext = pl.BlockSpec((group * t_ext + HALO, WIDTH_B), lambda i: (i, 0))
    mem = pl.BlockSpec((group, N_MEM, N_HEADS_X, HEAD_DIM_X), lambda i: (i, 0, 0, 0))
    vec = lambda w: _const_spec((1, w))
    return pl.pallas_call(
        functools.partial(_mix_sample_kernel, group=group, t_new=t_new, t_ext=t_ext),
        grid=(batch // group,),
        in_specs=[row(D_MODEL), ext, row(WIDTH_A), mem, mem, _const_spec(cw.shape),
                  vec(WIDTH_B), vec(WIDTH_B), vec(WIDTH_B), _const_spec(wout.shape), vec(D_MODEL),
                  _const_spec(wq.shape), _const_spec(wo.shape)],
        out_specs=row(D_MODEL),
        out_shape=jax.ShapeDtypeStruct((batch * t_new, D_MODEL), F32),
        scratch_shapes=[pltpu.VMEM((SUBLANES - 1, _shifted_rows(0, group * t_ext), WIDTH_B), F32)],
        compiler_params=_params("parallel"),
        name="mix_sample",
    )(x, uext, oa, mk, mv, cw, cb, lg, lb, wout, gx, wq, wo)


def kernel(x_prompt, x_sample, mem_prompt, cache_win_k, cache_win_v, cache_conv, cache_mem_k, cache_mem_v, ffn1_norm, ffn1_gate, ffn1_up, ffn1_down, mix_norm, w_in, conv_w, conv_b, conv_ln_g, conv_ln_b, w_out, xattn_norm, mem_norm, w_cq, w_ck, w_cv, w_co, ffn2_norm, ffn2_gate, ffn2_up, ffn2_down, final_norm):
    depth = ffn1_norm.shape[0]
    assert depth == 1
    bp, seq, _ = x_prompt.shape
    bs, t_new, _ = x_sample.shape
    keep = CONV_WIDTH - 1
    l = 0
    vec = lambda a: a.reshape(1, -1)
    bf = lambda a: a.astype(BF16)
    slopes = jnp.asarray([2.0 ** -(h + 1) for h in range(N_HEADS_A)], F32)

    f1 = (vec(ffn1_norm[l]), bf(ffn1_gate[l]), bf(ffn1_up[l]), bf(ffn1_down[l]))
    f2 = (vec(ffn2_norm[l]), bf(ffn2_gate[l]), bf(ffn2_up[l]), bf(ffn2_down[l]))
    gmix, win = vec(mix_norm[l]), bf(w_in[l])
    conv = (conv_w[l], vec(conv_b[l]), vec(conv_ln_g[l]), vec(conv_ln_b[l]))
    proj = (bf(w_out[l]), vec(xattn_norm[l]), bf(w_cq[l]), bf(w_co[l]))
    gfin = vec(final_norm)

    xp = x_prompt.reshape(bp * seq, D_MODEL)
    x1, q, k, v, u, kt, vt = _ffn_proj(xp, *f1, gmix, win, tm=512, seq=seq)
    oa = _attn_prompt(q, k, v, slopes, bp, seq)
    mk, mv, mkh, mvh = _mem_kv(mem_prompt.reshape(bp * N_MEM, D_MODEL), vec(mem_norm[l]), bf(w_ck[l]), bf(w_cv[l]),
                               tm=512)
    x3 = _mix_prompt(x1, u, oa, mk, mv, *conv, *proj, batch=bp, seq=seq, ts=512)
    yp = _ffn_final(x3, *f2, gfin, tm=1024)

    xs = x_sample.reshape(bs * t_new, D_MODEL)
    s1, sq, sk, sv, su, skh, svh = _ffn_proj(xs, *f1, gmix, win, tm=bs * t_new)
    seq_minor = lambda c: jnp.transpose(c, (0, 2, 3, 1)).reshape(c.shape[0], WIDTH_A, c.shape[1])
    soa = _attn_sample(sq, sk, sv, seq_minor(cache_win_k[l]), seq_minor(cache_win_v[l]), t_new)
    u_ext = jnp.concatenate([cache_conv[l], su.reshape(bs, t_new, WIDTH_B)], axis=1)
    t_ext = -(-(keep + t_new) // 8) * 8
    group = 8
    u_pad = jnp.pad(u_ext, ((0, 0), (0, t_ext - keep - t_new), (0, 0))).reshape(bs // group, group * t_ext, WIDTH_B)
    u_pad = jnp.pad(u_pad, ((0, 0), (0, HALO), (0, 0))).reshape(-1, WIDTH_B)
    s3 = _mix_sample(s1, u_pad, soa, cache_mem_k[l], cache_mem_v[l], *conv, *proj,
                     batch=bs, t_new=t_new, t_ext=t_ext, group=group)
    ys = _ffn_final(s3, *f2, gfin, tm=bs * t_new)

    heads = lambda a, b, t: a.reshape(1, b, t, N_HEADS_A, HEAD_DIM)
    from_seq_minor = lambda a: jnp.transpose(a.reshape(1, bp, N_HEADS_A, HEAD_DIM, seq), (0, 1, 4, 2, 3))
    mem_heads = lambda a: a.reshape(1, bp, N_MEM, N_HEADS_X, HEAD_DIM_X)
    return (yp.reshape(bp, seq, D_MODEL),
            ys.reshape(bs, t_new, D_MODEL),
            from_seq_minor(kt), from_seq_minor(vt),
            u.reshape(1, bp, seq, WIDTH_B)[:, :, seq - keep:],
            mem_heads(mkh), mem_heads(mvh),
            heads(skh, bs, t_new), heads(svh, bs, t_new),
            u_ext[None, :, t_new:])
```

```python
import functools

import jax
import jax.numpy as jnp
from jax import lax
from jax.experimental import pallas as pl
from jax.experimental.pallas import tpu as pltpu

D_MODEL = 1024
HEAD_DIM = 64
N_HEADS_A = 8
WIDTH_A = N_HEADS_A * HEAD_DIM
WIDTH_B = D_MODEL - WIDTH_A
DILATED_BRANCHES = ((128, 1), (512, 4), (2048, 16))
BAND = 128
CONV_WIDTH = 31
D_FF = 2816
N_MEM = 256
N_HEADS_X = 4
HEAD_DIM_X = D_MODEL // N_HEADS_X
EPS = 1e-6

LANES = 128
SUBLANES = 8
FF_CHUNK = 256
HALO = 32
ATTN_GROUP = 1
NEG_BIG = -1e30
VMEM_LIMIT = 56 * 1024 * 1024

F32 = jnp.float32
BF16 = jnp.bfloat16


def _const_spec(shape):
    nd = len(shape)
    return pl.BlockSpec(shape, lambda *_: (0,) * nd, pipeline_mode=pl.Buffered(1))


def _params(*sem):
    return pltpu.CompilerParams(dimension_semantics=sem, vmem_limit_bytes=VMEM_LIMIT)


def _rms(x, g):
    return x * lax.rsqrt(jnp.mean(x * x, axis=-1, keepdims=True) + EPS) * g


def _mm(a, b):
    return jnp.dot(a, b, preferred_element_type=F32)


def _mm_nt(a, b):
    return lax.dot_general(a, b, (((1,), (1,)), ((), ())), preferred_element_type=F32)


def _swiglu_half_step(x, g_ref, wg_ref, wu_ref, wd_ref, act_ref):
    h = _rms(x, g_ref[...]).astype(BF16)
    for c in range(0, D_FF, FF_CHUNK):
        gate = _mm(h, wg_ref[:, c:c + FF_CHUNK])
        up = _mm(h, wu_ref[:, c:c + FF_CHUNK])
        act_ref[:, c:c + FF_CHUNK] = (gate * jax.nn.sigmoid(gate) * up).astype(BF16)
    return x + 0.5 * _mm(act_ref[...], wd_ref[...])


def _ffn_proj_kernel(x_ref, g_ref, wg_ref, wu_ref, wd_ref, gmix_ref, win_ref,
                     x1_ref, q_ref, k_ref, v_ref, u_ref, kh_ref, vh_ref, act_ref):
    def put_cols(ref, val):
        if len(ref.shape) == 2:
            ref[...] = val
        else:
            for p in range(ref.shape[0]):
                ref[p] = val[:, p * LANES:(p + 1) * LANES]

    x1 = _swiglu_half_step(x_ref[...], g_ref, wg_ref, wu_ref, wd_ref, act_ref)
    x1_ref[...] = x1
    h = _rms(x1, gmix_ref[...]).astype(BF16)
    w = WIDTH_A
    put_cols(q_ref, _mm(h, win_ref[:, 0:w]) * (HEAD_DIM ** -0.5))
    k = _mm(h, win_ref[:, w:2 * w])
    v = _mm(h, win_ref[:, 2 * w:3 * w])
    put_cols(k_ref, k)
    put_cols(v_ref, v)
    if len(kh_ref.shape) == 2:
        kh_ref[...] = k.T
        vh_ref[...] = v.T
    else:
        kh_ref[...] = k.reshape(kh_ref.shape)
        vh_ref[...] = v.reshape(vh_ref.shape)
    a = _mm(h, win_ref[:, 3 * w:3 * w + WIDTH_B])
    g = _mm(h, win_ref[:, 3 * w + WIDTH_B:3 * w + 2 * WIDTH_B])
    u_ref[...] = a * jax.nn.sigmoid(g)


def _ffn_proj(x, g, wg, wu, wd, gmix, win, tm, seq=None):
    n = x.shape[0]
    row = lambda w: pl.BlockSpec((tm, w), lambda i: (i, 0))
    if seq is None:
        heads = pl.BlockSpec((tm, N_HEADS_A, HEAD_DIM), lambda i: (i, 0, 0))
        heads_shape = jax.ShapeDtypeStruct((n, N_HEADS_A, HEAD_DIM), F32)
        qkv = row(WIDTH_A)
        qkv_shape = jax.ShapeDtypeStruct((n, WIDTH_A), F32)
    else:
        tiles = seq // tm
        heads = pl.BlockSpec((None, WIDTH_A, tm), lambda i: (i // tiles, 0, i % tiles))
        heads_shape = jax.ShapeDtypeStruct((n // seq, WIDTH_A, seq), F32)
        qkv = pl.BlockSpec((WIDTH_A // LANES, tm, LANES), lambda i: (0, i, 0))
        qkv_shape = jax.ShapeDtypeStruct((WIDTH_A // LANES, n, LANES), F32)
    return pl.pallas_call(
        _ffn_proj_kernel,
        grid=(n // tm,),
        in_specs=[row(D_MODEL), _const_spec((1, D_MODEL)), _const_spec(wg.shape), _const_spec(wu.shape),
                  _const_spec(wd.shape), _const_spec((1, D_MODEL)), _const_spec(win.shape)],
        out_specs=[row(D_MODEL), qkv, qkv, qkv, row(WIDTH_B), heads, heads],
        out_shape=[jax.ShapeDtypeStruct((n, D_MODEL), F32)] + [qkv_shape] * 3
        + [jax.ShapeDtypeStruct((n, WIDTH_B), F32)] + [heads_shape] * 2,
        scratch_shapes=[pltpu.VMEM((tm, D_FF), BF16)],
        compiler_params=_params("parallel"),
        name="ffn_proj",
    )(x, g, wg, wu, wd, gmix, win)


def _ffn_final_kernel(x_ref, g_ref, wg_ref, wu_ref, wd_ref, gfin_ref, y_ref, act_ref):
    x1 = _swiglu_half_step(x_ref[...], g_ref, wg_ref, wu_ref, wd_ref, act_ref)
    y_ref[...] = _rms(x1, gfin_ref[...])


def _ffn_final(x, g, wg, wu, wd, gfin, tm):
    n = x.shape[0]
    row = pl.BlockSpec((tm, D_MODEL), lambda i: (i, 0))
    return pl.pallas_call(
        _ffn_final_kernel,
        grid=(n // tm,),
        in_specs=[row, _const_spec((1, D_MODEL)), _const_spec(wg.shape), _const_spec(wu.shape),
                  _const_spec(wd.shape), _const_spec((1, D_MODEL))],
        out_specs=row,
        out_shape=jax.ShapeDtypeStruct((n, D_MODEL), F32),
        scratch_shapes=[pltpu.VMEM((tm, D_FF), BF16)],
        compiler_params=_params("parallel"),
        name="ffn_final",
    )(x, g, wg, wu, wd, gfin)


def _attn_prompt_kernel(slopes_ref, q_ref, k_ref, v_ref, o_ref,
                        x4_ref, qs_ref, kp_ref, vp_ref, bias_ref, s_ref, p_ref, m_ref, l_ref, n_ref,
                        fm_ref, fl_ref, fn_ref, *, seq):
    pair = pl.program_id(0)
    nb = seq // BAND
    lo = lax.broadcasted_iota(jnp.int32, (BAND, LANES), 1) < HEAD_DIM

    @pl.when(pl.program_id(1) == 0)
    def _():
        kp_ref[:, 0:BAND, :] = jnp.zeros((3, BAND, LANES), BF16)
        vp_ref[:, 0:BAND, 0:LANES] = jnp.zeros((3, BAND, LANES), BF16)
        vp_ref[:, :, LANES:2 * LANES] = jnp.ones((3, BAND + seq, LANES), BF16)
        qi = lax.broadcasted_iota(jnp.int32, (BAND, 2 * BAND), 0)
        kj = lax.broadcasted_iota(jnp.int32, (BAND, 2 * BAND), 1)
        dist = qi + BAND - kj
        neg_dist = jnp.where((dist >= 0) & (dist <= BAND), -dist.astype(F32), NEG_BIG)
        neg_dist_cur = jnp.where(kj >= BAND, neg_dist, NEG_BIG)
        for b, (_, dil) in enumerate(DILATED_BRANCHES):
            for noprev in range(2):
                if 2 * b + noprev < bias_ref.shape[0]:
                    table = neg_dist_cur if noprev else neg_dist
                    for half in range(2):
                        bias_ref[2 * b + noprev, half * BAND:(half + 1) * BAND, :] = (
                            (table * slopes_ref[2 * pair + half]) * float(dil))

    def put(kind, b, first, x):
        nblocks = x.shape[0] // BAND
        if kind == 0:
            is_lo = lax.broadcasted_iota(jnp.int32, x.shape, 1) < HEAD_DIM
            x_lo = jnp.where(is_lo, x, 0.0).astype(BF16)
            x_hi = jnp.where(is_lo, 0.0, x).astype(BF16)
            for j in range(nblocks):
                qs_ref[b, first + j, 0] = x_lo[j * BAND:(j + 1) * BAND]
                qs_ref[b, first + j, 1] = x_hi[j * BAND:(j + 1) * BAND]
        elif kind == 1:
            kp_ref[b, BAND + first * BAND:BAND + first * BAND + x.shape[0], :] = x.astype(BF16)
        else:
            vp_ref[b, BAND + first * BAND:BAND + first * BAND + x.shape[0], 0:LANES] = x.astype(BF16)

    quarter = seq // 4
    for kind, ref in enumerate((q_ref, k_ref, v_ref)):
        for r in range(4):
            put(kind, 0, 4 * r, ref[r * quarter:(r + 1) * quarter, :])
            x = ref[pl.ds(r, quarter, stride=4), :]
            x4_ref[r * quarter:(r + 1) * quarter, :] = x
            put(kind, 1, 4 * r, x)
        for r in range(16):
            put(kind, 2, r, x4_ref[pl.ds((r % 4) * quarter + r // 4, seq // 16, stride=4), :])

    def blocks_per_class(b):
        return seq // DILATED_BRANCHES[b][1] // BAND

    def keys_of(b):
        return 2 * BAND if blocks_per_class(b) > 1 else BAND

    def scores(n):
        for b in range(3):
            qb = qs_ref[b, n].reshape(2 * BAND, LANES)
            first = n * BAND + (2 * BAND - keys_of(b))
            s_ref[n, b, :, 0:keys_of(b)] = _mm_nt(qb, kp_ref[b, first:first + keys_of(b), :])

    def softmax(n):
        for b in range(3):
            if blocks_per_class(b) > 1:
                bias = bias_ref[2 * b + int(n % blocks_per_class(b) == 0)]
            else:
                bias = bias_ref[2 * b, :, BAND:2 * BAND]
            s = s_ref[n, b, :, 0:keys_of(b)] + bias
            m = jnp.max(s, axis=-1, keepdims=True)
            p_ref[n, b, :, 0:keys_of(b)] = jnp.exp(s - m).astype(BF16)
            m_ref[b, n * BAND:(n + 1) * BAND, :] = jnp.where(lo, m[0:BAND], m[BAND:2 * BAND])

    def weighted_values(n):
        for b in range(3):
            first = n * BAND + (2 * BAND - keys_of(b))
            pv = _mm(p_ref[n, b, :, 0:keys_of(b)], vp_ref[b, first:first + keys_of(b), :])
            rows = slice(n * BAND, (n + 1) * BAND)
            n_ref[b, rows, :] = jnp.where(lo, pv[0:BAND, 0:LANES], pv[BAND:2 * BAND, 0:LANES])
            l_ref[b, rows, :] = jnp.where(lo, pv[0:BAND, LANES:2 * LANES], pv[BAND:2 * BAND, LANES:2 * LANES])

    groups = nb // ATTN_GROUP
    group = lambda i: range(i * ATTN_GROUP, (i + 1) * ATTN_GROUP) if 0 <= i < groups else ()
    for n in group(0):
        scores(n)
    for i in range(groups + 1):
        for n in group(i - 1):
            weighted_values(n)
        for n in group(i + 1):
            scores(n)
        for n in group(i):
            softmax(n)

    per16 = seq // 16
    for r in range(16):
        sl4 = pl.ds((r % 4) * quarter + r // 4, per16, stride=4)
        blk = pl.ds(r * per16, per16)
        m1, m2 = m_ref[1, sl4, :], m_ref[2, blk, :]
        mx = jnp.maximum(m1, m2)
        e1, e2 = jnp.exp(m1 - mx), jnp.exp(m2 - mx)
        fm_ref[sl4, :] = mx
        fl_ref[sl4, :] = e1 * l_ref[1, sl4, :] + e2 * l_ref[2, blk, :]
        fn_ref[sl4, :] = e1 * n_ref[1, sl4, :] + e2 * n_ref[2, blk, :]
    for r in range(4):
        for c in range(quarter // BAND):
            sl = pl.ds(r + 4 * BAND * c, BAND, stride=4)
            blk = pl.ds(r * quarter + c * BAND, BAND)
            m0, m1 = m_ref[0, sl, :], fm_ref[blk, :]
            mx = jnp.maximum(m0, m1)
            e0, e1 = jnp.exp(m0 - mx), jnp.exp(m1 - mx)
            den = e0 * l_ref[0, sl, :] + e1 * fl_ref[blk, :]
            num = e0 * n_ref[0, sl, :] + e1 * fn_ref[blk, :]
            o_ref[sl, :] = num / den


def _attn_prompt(q, k, v, slopes, batch, seq):
    assert [seq // d // BAND for _, d in DILATED_BRANCHES] == [16, 4, 1]
    nb = seq // BAND
    blk = pl.BlockSpec((None, seq, LANES), lambda p, b: (p, b, 0))
    return pl.pallas_call(
        functools.partial(_attn_prompt_kernel, seq=seq),
        grid=(WIDTH_A // LANES, batch),
        in_specs=[pl.BlockSpec(memory_space=pltpu.SMEM), blk, blk, blk],
        out_specs=blk,
        out_shape=jax.ShapeDtypeStruct((WIDTH_A // LANES, batch * seq, LANES), F32),
        scratch_shapes=[pltpu.VMEM((seq, LANES), F32),
                        pltpu.VMEM((3, nb, 2, BAND, LANES), BF16),
                        pltpu.VMEM((3, BAND + seq, LANES), BF16),
                        pltpu.VMEM((3, BAND + seq, 2 * LANES), BF16),
                        pltpu.VMEM((5, 2 * BAND, 2 * BAND), F32),
                        pltpu.VMEM((nb, 3, 2 * BAND, 2 * BAND), F32),
                        pltpu.VMEM((nb, 3, 2 * BAND, 2 * BAND), BF16)]
        + [pltpu.VMEM((3, seq, LANES), F32)] * 3 + [pltpu.VMEM((seq, LANES), F32)] * 3,
        compiler_params=_params("parallel", "arbitrary"),
        name="attn_prompt",
    )(slopes, q, k, v)


def _attn_sample_kernel(q_ref, kn_ref, vn_ref, kt_ref, vt_ref, o_ref, *, t_new, w_buf, pad):
    rows = N_HEADS_A * t_new
    zeros = jnp.zeros((pad - t_new, WIDTH_A), F32)
    k_new = jnp.concatenate([kn_ref[...], zeros], axis=0).astype(BF16)
    v_new = jnp.concatenate([vn_ref[...], zeros], axis=0).astype(BF16)

    qrep = jnp.concatenate([q_ref[...]] * N_HEADS_A, axis=0)
    rr = lax.broadcasted_iota(jnp.int32, (rows, WIDTH_A), 0)
    ll = lax.broadcasted_iota(jnp.int32, (rows, WIDTH_A), 1)
    qrows = jnp.where(rr // t_new == ll // HEAD_DIM, qrep, 0.0).astype(BF16)
    s = jnp.concatenate([_mm(qrows, kt_ref[...].astype(BF16)), _mm_nt(qrows, k_new)], axis=1)

    r2 = lax.broadcasted_iota(jnp.int32, s.shape, 0)
    u2 = lax.broadcasted_iota(jnp.int32, s.shape, 1)
    d = w_buf + r2 % t_new - u2
    mult = jnp.zeros(s.shape, F32)
    for window, dil in DILATED_BRANCHES:
        mult = mult + jnp.where((d >= 0) & (d <= window) & (d % dil == 0), 1.0, 0.0)
    head = lax.broadcasted_iota(jnp.int32, (rows, 1), 0) // t_new
    slope = jnp.zeros((rows, 1), F32)
    for h in range(N_HEADS_A):
        slope = jnp.where(head == h, 2.0 ** -(h + 1), slope)
    s = jnp.where(mult > 0, s - d.astype(F32) * slope, NEG_BIG)
    m = jnp.max(s, axis=-1, keepdims=True)
    p = mult * jnp.exp(s - m)
    den = jnp.sum(p, axis=-1, keepdims=True)
    p = p.astype(BF16)
    o = (_mm_nt(p[:, 0:w_buf], vt_ref[...].astype(BF16)) + _mm(p[:, w_buf:w_buf + pad], v_new)) * (1.0 / den)

    lane_head = lax.broadcasted_iota(jnp.int32, (t_new, WIDTH_A), 1) // HEAD_DIM
    out = jnp.zeros((t_new, WIDTH_A), F32)
    for h in range(N_HEADS_A):
        out = jnp.where(lane_head == h, o[h * t_new:(h + 1) * t_new, :], out)
    o_ref[...] = out


def _attn_sample(q, k, v, cache_kt, cache_vt, t_new):
    batch, _, w_buf = cache_kt.shape
    assert t_new % SUBLANES == 0 and w_buf >= DILATED_BRANCHES[-1][0] and w_buf % LANES == 0
    pad = LANES
    new = pl.BlockSpec((t_new, WIDTH_A), lambda b: (b, 0))
    cache = pl.BlockSpec((None, WIDTH_A, w_buf), lambda b: (b, 0, 0))
    return pl.pallas_call(
        functools.partial(_attn_sample_kernel, t_new=t_new, w_buf=w_buf, pad=pad),
        grid=(batch,),
        in_specs=[new, new, new, cache, cache],
        out_specs=new,
        out_shape=jax.ShapeDtypeStruct((batch * t_new, WIDTH_A), F32),
        compiler_params=_params("parallel"),
        name="attn_sample",
    )(q, k, v, cache_kt, cache_vt)


def _mem_kv_kernel(mem_ref, g_ref, wk_ref, wv_ref, mk_ref, mv_ref, mkh_ref, mvh_ref):
    h = _rms(mem_ref[...], g_ref[...]).astype(BF16)
    mk = _mm(h, wk_ref[...])
    mv = _mm(h, wv_ref[...])
    mk_ref[...] = mk.astype(BF16)
    mv_ref[...] = mv.astype(BF16)
    mkh_ref[...] = mk.reshape(mkh_ref.shape)
    mvh_ref[...] = mv.reshape(mvh_ref.shape)


def _mem_kv(mem, g, wk, wv, tm):
    n = mem.shape[0]
    row = pl.BlockSpec((tm, D_MODEL), lambda i: (i, 0))
    heads = pl.BlockSpec((tm, N_HEADS_X, HEAD_DIM_X), lambda i: (i, 0, 0))
    return pl.pallas_call(
        _mem_kv_kernel,
        grid=(n // tm,),
        in_specs=[row, _const_spec((1, D_MODEL)), _const_spec(wk.shape), _const_spec(wv.shape)],
        out_specs=[row, row, heads, heads],
        out_shape=[jax.ShapeDtypeStruct((n, D_MODEL), BF16)] * 2
        + [jax.ShapeDtypeStruct((n, N_HEADS_X, HEAD_DIM_X), F32)] * 2,
        compiler_params=_params("parallel"),
        name="mem_kv",
    )(mem, g, wk, wv)


def _shifted_rows(first_row, rows):
    offs = [first_row + j for j in range(CONV_WIDTH) if (first_row + j) % SUBLANES]
    return max(offs) // SUBLANES * SUBLANES + rows


def _realign_conv_input(ext_ref, sh_ref):
    for s in range(1, SUBLANES):
        sh_ref[s - 1] = ext_ref[pl.ds(s, sh_ref.shape[1]), :]


def _conv_module(ext_ref, sh_ref, first_row, row0, rows, cw_ref, cb_ref, lg_ref, lb_ref):
    y = cb_ref[...]
    for j in range(CONV_WIDTH):
        base, s = (first_row + j) // SUBLANES * SUBLANES, (first_row + j) % SUBLANES
        at = pl.ds(base + row0, rows)
        y = y + cw_ref[j:j + 1, :] * (ext_ref[at, :] if s == 0 else sh_ref[s - 1, at, :])
    yc = y - jnp.mean(y, axis=-1, keepdims=True)
    yn = yc * lax.rsqrt(jnp.mean(yc * yc, axis=-1, keepdims=True) + EPS) * lg_ref[...] + lb_ref[...]
    return yn * jax.nn.sigmoid(yn)


def _softmax_rows(s):
    m = jnp.max(s, axis=-1, keepdims=True)
    p = jnp.exp(s - m)
    return p.astype(BF16), 1.0 / jnp.sum(p, axis=-1, keepdims=True)


def _mix_prompt_kernel(x_ref, u_ref, halo_ref, oa_ref, mk_ref, mv_ref, cw_ref, cb_ref, lg_ref, lb_ref,
                       wout_ref, gx_ref, wq_ref, wo_ref, y_ref, ext_ref, sh_ref):
    ts = x_ref.shape[0]
    halo = halo_ref[...]
    ext_ref[0:HALO, :] = jnp.where(pl.program_id(1) == 0, jnp.zeros_like(halo), halo)
    ext_ref[HALO:HALO + ts, :] = u_ref[...]
    _realign_conv_input(ext_ref, sh_ref)
    ob = _conv_module(ext_ref, sh_ref, HALO - (CONV_WIDTH - 1), 0, ts, cw_ref, cb_ref, lg_ref, lb_ref)
    oa = jnp.concatenate([oa_ref[p] for p in range(oa_ref.shape[0])], axis=-1)
    x2 = (x_ref[...] + _mm(oa.astype(BF16), wout_ref[0:WIDTH_A, :])
          + _mm(ob.astype(BF16), wout_ref[WIDTH_A:WIDTH_A + WIDTH_B, :]))
    hx = _rms(x2, gx_ref[...]).astype(BF16)
    qx = (_mm(hx, wq_ref[...]) * (HEAD_DIM_X ** -0.5)).astype(BF16)
    outs = []
    for h in range(N_HEADS_X):
        sl = slice(h * HEAD_DIM_X, (h + 1) * HEAD_DIM_X)
        p, inv = _softmax_rows(_mm_nt(qx[:, sl], mk_ref[:, sl]))
        outs.append((_mm(p, mv_ref[:, sl]) * inv).astype(BF16))
    y_ref[...] = x2 + _mm(jnp.concatenate(outs, axis=-1), wo_ref[...])


def _mix_prompt(x, u, oa, mk, mv, cw, cb, lg, lb, wout, gx, wq, wo, batch, seq, ts):
    tiles = seq // ts
    row = lambda w: pl.BlockSpec((ts, w), lambda b, i: (b * tiles + i, 0))
    oa_spec = pl.BlockSpec((WIDTH_A // LANES, ts, LANES), lambda b, i: (0, b * tiles + i, 0))
    halo = pl.BlockSpec((HALO, WIDTH_B),
                        lambda b, i: (jnp.maximum((b * seq + i * ts) // HALO - 1, 0), 0))
    mem = pl.BlockSpec((N_MEM, D_MODEL), lambda b, i: (b, 0))
    vec = lambda w: _const_spec((1, w))
    return pl.pallas_call(
        _mix_prompt_kernel,
        grid=(batch, tiles),
        in_specs=[row(D_MODEL), row(WIDTH_B), halo, oa_spec, mem, mem, _const_spec(cw.shape),
                  vec(WIDTH_B), vec(WIDTH_B), vec(WIDTH_B), _const_spec(wout.shape), vec(D_MODEL),
                  _const_spec(wq.shape), _const_spec(wo.shape)],
        out_specs=row(D_MODEL),
        out_shape=jax.ShapeDtypeStruct((batch * seq, D_MODEL), F32),
        scratch_shapes=[pltpu.VMEM((HALO + ts, WIDTH_B), F32),
                        pltpu.VMEM((SUBLANES - 1, _shifted_rows(HALO - (CONV_WIDTH - 1), ts), WIDTH_B), F32)],
        compiler_params=_params("parallel", "parallel"),
        name="mix_prompt",
    )(x, u, u, oa, mk, mv, cw, cb, lg, lb, wout, gx, wq, wo)


def _mix_sample_kernel(x_ref, uext_ref, oa_ref, mk_ref, mv_ref, cw_ref, cb_ref, lg_ref, lb_ref,
                       wout_ref, gx_ref, wq_ref, wo_ref, y_ref, sh_ref, *, group, t_new, t_ext):
    rows = group * t_ext
    _realign_conv_input(uext_ref, sh_ref)
    conv = _conv_module(uext_ref, sh_ref, 0, 0, rows, cw_ref, cb_ref, lg_ref, lb_ref)
    ob = conv.reshape(group, t_ext, WIDTH_B)[:, 0:t_new, :].reshape(group * t_new, WIDTH_B)
    x2 = (x_ref[...] + _mm(oa_ref[...].astype(BF16), wout_ref[0:WIDTH_A, :])
          + _mm(ob.astype(BF16), wout_ref[WIDTH_A:WIDTH_A + WIDTH_B, :]))

    hx = _rms(x2, gx_ref[...]).astype(BF16)
    qx = _mm(hx, wq_ref[...]) * (HEAD_DIM_X ** -0.5)
    qrows_n = N_HEADS_X * t_new
    rr = lax.broadcasted_iota(jnp.int32, (qrows_n, D_MODEL), 0)
    ll = lax.broadcasted_iota(jnp.int32, (qrows_n, D_MODEL), 1)
    own = rr // t_new == ll // HEAD_DIM_X
    lane_head = lax.broadcasted_iota(jnp.int32, (t_new, D_MODEL), 1) // HEAD_DIM_X
    outs = []
    for b in range(group):
        qb = qx[b * t_new:(b + 1) * t_new, :]
        qrows = jnp.where(own, jnp.concatenate([qb] * N_HEADS_X, axis=0), 0.0).astype(BF16)
        mk = mk_ref[b].reshape(N_MEM, D_MODEL).astype(BF16)
        mv = mv_ref[b].reshape(N_MEM, D_MODEL).astype(BF16)
        p, inv = _softmax_rows(_mm_nt(qrows, mk))
        o = _mm(p, mv) * inv
        out = jnp.zeros((t_new, D_MODEL), F32)
        for h in range(N_HEADS_X):
            out = jnp.where(lane_head == h, o[h * t_new:(h + 1) * t_new, :], out)
        outs.append(out)
    y_ref[...] = x2 + _mm(jnp.concatenate(outs, axis=0).astype(BF16), wo_ref[...])


def _mix_sample(x, uext, oa, mk, mv, cw, cb, lg, lb, wout, gx, wq, wo, batch, t_new, t_ext, group):
    rows = group * t_new
    row = lambda w: pl.BlockSpec((rows, w), lambda i: (i, 0))
    ext = pl.BlockSpec((group * t_ext + HALO, WIDTH_B), lambda i: (i, 0))
    mem = pl.BlockSpec((group, N_MEM, N_HEADS_X, HEAD_DIM_X), lambda i: (i, 0, 0, 0))
    vec = lambda w: _const_spec((1, w))
    return pl.pallas_call(
        functools.partial(_mix_sample_kernel, group=group, t_new=t_new, t_ext=t_ext),
        grid=(batch // group,),
        in_specs=[row(D_MODEL), ext, row(WIDTH_A), mem, mem, _const_spec(cw.shape),
                  vec(WIDTH_B), vec(WIDTH_B), vec(WIDTH_B), _const_spec(wout.shape), vec(D_MODEL),
                  _const_spec(wq.shape), _const_spec(wo.shape)],
        out_specs=row(D_MODEL),
        out_shape=jax.ShapeDtypeStruct((batch * t_new, D_MODEL), F32),
        scratch_shapes=[pltpu.VMEM((SUBLANES - 1, _shifted_rows(0, group * t_ext), WIDTH_B), F32)],
        compiler_params=_params("parallel"),
        name="mix_sample",
    )(x, uext, oa, mk, mv, cw, cb, lg, lb, wout, gx, wq, wo)


def kernel(x_prompt, x_sample, mem_prompt, cache_win_k, cache_win_v, cache_conv, cache_mem_k, cache_mem_v, ffn1_norm, ffn1_gate, ffn1_up, ffn1_down, mix_norm, w_in, conv_w, conv_b, conv_ln_g, conv_ln_b, w_out, xattn_norm, mem_norm, w_cq, w_ck, w_cv, w_co, ffn2_norm, ffn2_gate, ffn2_up, ffn2_down, final_norm):
    depth = ffn1_norm.shape[0]
    assert depth == 1
    bp, seq, _ = x_prompt.shape
    bs, t_new, _ = x_sample.shape
    keep = CONV_WIDTH - 1
    l = 0
    vec = lambda a: a.reshape(1, -1)
    bf = lambda a: a.astype(BF16)
    slopes = jnp.asarray([2.0 ** -(h + 1) for h in range(N_HEADS_A)], F32)

    f1 = (vec(ffn1_norm[l]), bf(ffn1_gate[l]), bf(ffn1_up[l]), bf(ffn1_down[l]))
    f2 = (vec(ffn2_norm[l]), bf(ffn2_gate[l]), bf(ffn2_up[l]), bf(ffn2_down[l]))
    gmix, win = vec(mix_norm[l]), bf(w_in[l])
    conv = (conv_w[l], vec(conv_b[l]), vec(conv_ln_g[l]), vec(conv_ln_b[l]))
    proj = (bf(w_out[l]), vec(xattn_norm[l]), bf(w_cq[l]), bf(w_co[l]))
    gfin = vec(final_norm)

    xp = x_prompt.reshape(bp * seq, D_MODEL)
    x1, q, k, v, u, kt, vt = _ffn_proj(xp, *f1, gmix, win, tm=512, seq=seq)
    oa = _attn_prompt(q, k, v, slopes, bp, seq)
    mk, mv, mkh, mvh = _mem_kv(mem_prompt.reshape(bp * N_MEM, D_MODEL), vec(mem_norm[l]), bf(w_ck[l]), bf(w_cv[l]),
                               tm=512)
    x3 = _mix_prompt(x1, u, oa, mk, mv, *conv, *proj, batch=bp, seq=seq, ts=512)
    yp = _ffn_final(x3, *f2, gfin, tm=1024)

    xs = x_sample.reshape(bs * t_new, D_MODEL)
    s1, sq, sk, sv, su, skh, svh = _ffn_proj(xs, *f1, gmix, win, tm=bs * t_new)
    seq_minor = lambda c: jnp.transpose(c, (0, 2, 3, 1)).reshape(c.shape[0], WIDTH_A, c.shape[1])
    soa = _attn_sample(sq, sk, sv, seq_minor(cache_win_k[l]), seq_minor(cache_win_v[l]), t_new)
    u_ext = jnp.concatenate([cache_conv[l], su.reshape(bs, t_new, WIDTH_B)], axis=1)
    t_ext = -(-(keep + t_new) // 8) * 8
    group = 8
    u_pad = jnp.pad(u_ext, ((0, 0), (0, t_ext - keep - t_new), (0, 0))).reshape(bs // group, group * t_ext, WIDTH_B)
    u_pad = jnp.pad(u_pad, ((0, 0), (0, HALO), (0, 0))).reshape(-1, WIDTH_B)
    s3 = _mix_sample(s1, u_pad, soa, cache_mem_k[l], cache_mem_v[l], *conv, *proj,
                     batch=bs, t_new=t_new, t_ext=t_ext, group=group)
    ys = _ffn_final(s3, *f2, gfin, tm=bs * t_new)

    heads = lambda a, b, t: a.reshape(1, b, t, N_HEADS_A, HEAD_DIM)
    from_seq_minor = lambda a: jnp.transpose(a.reshape(1, bp, N_HEADS_A, HEAD_DIM, seq), (0, 1, 4, 2, 3))
    mem_heads = lambda a: a.reshape(1, bp, N_MEM, N_HEADS_X, HEAD_DIM_X)
    return (yp.reshape(bp, seq, D_MODEL),
            ys.reshape(bs, t_new, D_MODEL),
            from_seq_minor(kt), from_seq_minor(vt),
            u.reshape(1, bp, seq, WIDTH_B)[:, :, seq - keep:],
            mem_heads(mkh), mem_heads(mvh),
            heads(skh, bs, t_new), heads(svh, bs, t_new),
            u_ext[None, :, t_new:])
```

```python
import functools

import jax
import jax.numpy as jnp
from jax import lax
from jax.experimental import pallas as pl
from jax.experimental.pallas import tpu as pltpu

D_MODEL = 1024
HEAD_DIM = 64
N_HEADS_A = 8
WIDTH_A = N_HEADS_A * HEAD_DIM
WIDTH_B = D_MODEL - WIDTH_A
DILATED_BRANCHES = ((128, 1), (512, 4), (2048, 16))
BAND = 128
CONV_WIDTH = 31
D_FF = 2816
N_MEM = 256
N_HEADS_X = 4
HEAD_DIM_X = D_MODEL // N_HEADS_X
EPS = 1e-6

LANES = 128
SUBLANES = 8
FF_CHUNK = 256
HALO = 32
ATTN_GROUP = 1
NEG_BIG = -1e30
VMEM_LIMIT = 56 * 1024 * 1024

F32 = jnp.float32
BF16 = jnp.bfloat16


def _const_spec(shape):
    nd = len(shape)
    return pl.BlockSpec(shape, lambda *_: (0,) * nd, pipeline_mode=pl.Buffered(1))


def _params(*sem):
    return pltpu.CompilerParams(dimension_semantics=sem, vmem_limit_bytes=VMEM_LIMIT)


def _rms(x, g):
    return x * lax.rsqrt(jnp.mean(x * x, axis=-1, keepdims=True) + EPS) * g


def _mm(a, b):
    return jnp.dot(a, b, preferred_element_type=F32)


def _mm_nt(a, b):
    return lax.dot_general(a, b, (((1,), (1,)), ((), ())), preferred_element_type=F32)


def _swiglu_half_step(x, g_ref, wg_ref, wu_ref, wd_ref, act_ref):
    h = _rms(x, g_ref[...]).astype(BF16)
    for c in range(0, D_FF, FF_CHUNK):
        gate = _mm(h, wg_ref[:, c:c + FF_CHUNK])
        up = _mm(h, wu_ref[:, c:c + FF_CHUNK])
        act_ref[:, c:c + FF_CHUNK] = (gate * jax.nn.sigmoid(gate) * up).astype(BF16)
    return x + 0.5 * _mm(act_ref[...], wd_ref[...])


def _ffn_proj_kernel(x_ref, g_ref, wg_ref, wu_ref, wd_ref, gmix_ref, win_ref,
                     x1_ref, q_ref, k_ref, v_ref, u_ref, kh_ref, vh_ref, act_ref):
    def put_cols(ref, val):
        if len(ref.shape) == 2:
            ref[...] = val
        else:
            for p in range(ref.shape[0]):
                ref[p] = val[:, p * LANES:(p + 1) * LANES]

    x1 = _swiglu_half_step(x_ref[...], g_ref, wg_ref, wu_ref, wd_ref, act_ref)
    x1_ref[...] = x1
    h = _rms(x1, gmix_ref[...]).astype(BF16)
    w = WIDTH_A
    put_cols(q_ref, _mm(h, win_ref[:, 0:w]) * (HEAD_DIM ** -0.5))
    k = _mm(h, win_ref[:, w:2 * w])
    v = _mm(h, win_ref[:, 2 * w:3 * w])
    put_cols(k_ref, k)
    put_cols(v_ref, v)
    if len(kh_ref.shape) == 2:
        kh_ref[...] = k.T
        vh_ref[...] = v.T
    else:
        kh_ref[...] = k.reshape(kh_ref.shape)
        vh_ref[...] = v.reshape(vh_ref.shape)
    a = _mm(h, win_ref[:, 3 * w:3 * w + WIDTH_B])
    g = _mm(h, win_ref[:, 3 * w + WIDTH_B:3 * w + 2 * WIDTH_B])
    u_ref[...] = a * jax.nn.sigmoid(g)


def _ffn_proj(x, g, wg, wu, wd, gmix, win, tm, seq=None):
    n = x.shape[0]
    row = lambda w: pl.BlockSpec((tm, w), lambda i: (i, 0))
    if seq is None:
        heads = pl.BlockSpec((tm, N_HEADS_A, HEAD_DIM), lambda i: (i, 0, 0))
        heads_shape = jax.ShapeDtypeStruct((n, N_HEADS_A, HEAD_DIM), F32)
        qkv = row(WIDTH_A)
        qkv_shape = jax.ShapeDtypeStruct((n, WIDTH_A), F32)
    else:
        tiles = seq // tm
        heads = pl.BlockSpec((None, WIDTH_A, tm), lambda i: (i // tiles, 0, i % tiles))
        heads_shape = jax.ShapeDtypeStruct((n // seq, WIDTH_A, seq), F32)
        qkv = pl.BlockSpec((WIDTH_A // LANES, tm, LANES), lambda i: (0, i, 0))
        qkv_shape = jax.ShapeDtypeStruct((WIDTH_A // LANES, n, LANES), F32)
    return pl.pallas_call(
        _ffn_proj_kernel,
        grid=(n // tm,),
        in_specs=[row(D_MODEL), _const_spec((1, D_MODEL)), _const_spec(wg.shape), _const_spec(wu.shape),
                  _const_spec(wd.shape), _const_spec((1, D_MODEL)), _const_spec(win.shape)],
        out_specs=[row(D_MODEL), qkv, qkv, qkv, row(WIDTH_B), heads, heads],
        out_shape=[jax.ShapeDtypeStruct((n, D_MODEL), F32)] + [qkv_shape] * 3
        + [jax.ShapeDtypeStruct((n, WIDTH_B), F32)] + [heads_shape] * 2,
        scratch_shapes=[pltpu.VMEM((tm, D_FF), BF16)],
        compiler_params=_params("parallel"),
        name="ffn_proj",
    )(x, g, wg, wu, wd, gmix, win)


def _ffn_final_kernel(x_ref, g_ref, wg_ref, wu_ref, wd_ref, gfin_ref, y_ref, act_ref):
    x1 = _swiglu_half_step(x_ref[...], g_ref, wg_ref, wu_ref, wd_ref, act_ref)
    y_ref[...] = _rms(x1, gfin_ref[...])


def _ffn_final(x, g, wg, wu, wd, gfin, tm):
    n = x.shape[0]
    row = pl.BlockSpec((tm, D_MODEL), lambda i: (i, 0))
    return pl.pallas_call(
        _ffn_final_kernel,
        grid=(n // tm,),
        in_specs=[row, _const_spec((1, D_MODEL)), _const_spec(wg.shape), _const_spec(wu.shape),
                  _const_spec(wd.shape), _const_spec((1, D_MODEL))],
        out_specs=row,
        out_shape=jax.ShapeDtypeStruct((n, D_MODEL), F32),
        scratch_shapes=[pltpu.VMEM((tm, D_FF), BF16)],
        compiler_params=_params("parallel"),
        name="ffn_final",
    )(x, g, wg, wu, wd, gfin)


def _attn_prompt_kernel(slopes_ref, q_ref, k_ref, v_ref, o_ref,
                        x4_ref, qs_ref, kp_ref, vp_ref, bias_ref, s_ref, p_ref, m_ref, l_ref, n_ref,
                        fm_ref, fl_ref, fn_ref, *, seq):
    pair = pl.program_id(0)
    nb = seq // BAND
    lo = lax.broadcasted_iota(jnp.int32, (BAND, LANES), 1) < HEAD_DIM

    @pl.when(pl.program_id(1) == 0)
    def _():
        kp_ref[:, 0:BAND, :] = jnp.zeros((3, BAND, LANES), BF16)
        vp_ref[:, 0:BAND, 0:LANES] = jnp.zeros((3, BAND, LANES), BF16)
        vp_ref[:, :, LANES:2 * LANES] = jnp.ones((3, BAND + seq, LANES), BF16)
        qi = lax.broadcasted_iota(jnp.int32, (BAND, 2 * BAND), 0)
        kj = lax.broadcasted_iota(jnp.int32, (BAND, 2 * BAND), 1)
        dist = qi + BAND - kj
        neg_dist = jnp.where((dist >= 0) & (dist <= BAND), -dist.astype(F32), NEG_BIG)
        neg_dist_cur = jnp.where(kj >= BAND, neg_dist, NEG_BIG)
        for b, (_, dil) in enumerate(DILATED_BRANCHES):
            for noprev in range(2):
                if 2 * b + noprev < bias_ref.shape[0]:
                    table = neg_dist_cur if noprev else neg_dist
                    for half in range(2):
                        bias_ref[2 * b + noprev, half * BAND:(half + 1) * BAND, :] = (
                            (table * slopes_ref[2 * pair + half]) * float(dil))

    def put(kind, b, first, x):
        nblocks = x.shape[0] // BAND
        if kind == 0:
            is_lo = lax.broadcasted_iota(jnp.int32, x.shape, 1) < HEAD_DIM
            x_lo = jnp.where(is_lo, x, 0.0).astype(BF16)
            x_hi = jnp.where(is_lo, 0.0, x).astype(BF16)
            for j in range(nblocks):
                qs_ref[b, first + j, 0] = x_lo[j * BAND:(j + 1) * BAND]
                qs_ref[b, first + j, 1] = x_hi[j * BAND:(j + 1) * BAND]
        elif kind == 1:
            kp_ref[b, BAND + first * BAND:BAND + first * BAND + x.shape[0], :] = x.astype(BF16)
        else:
            vp_ref[b, BAND + first * BAND:BAND + first * BAND + x.shape[0], 0:LANES] = x.astype(BF16)

    quarter = seq // 4
    for kind, ref in enumerate((q_ref, k_ref, v_ref)):
        for r in range(4):
            put(kind, 0, 4 * r, ref[r * quarter:(r + 1) * quarter, :])
            x = ref[pl.ds(r, quarter, stride=4), :]
            x4_ref[r * quarter:(r + 1) * quarter, :] = x
            put(kind, 1, 4 * r, x)
        for r in range(16):
            put(kind, 2, r, x4_ref[pl.ds((r % 4) * quarter + r // 4, seq // 16, stride=4), :])

    def blocks_per_class(b):
        return seq // DILATED_BRANCHES[b][1] // BAND

    def keys_of(b):
        return 2 * BAND if blocks_per_class(b) > 1 else BAND

    def scores(n):
        for b in range(3):
            qb = qs_ref[b, n].reshape(2 * BAND, LANES)
            first = n * BAND + (2 * BAND - keys_of(b))
            s_ref[n, b, :, 0:keys_of(b)] = _mm_nt(qb, kp_ref[b, first:first + keys_of(b), :])

    def softmax(n):
        for b in range(3):
            if blocks_per_class(b) > 1:
                bias = bias_ref[2 * b + int(n % blocks_per_class(b) == 0)]
            else:
                bias = bias_ref[2 * b, :, BAND:2 * BAND]
            s = s_ref[n, b, :, 0:keys_of(b)] + bias
            m = jnp.max(s, axis=-1, keepdims=True)
            p_ref[n, b, :, 0:keys_of(b)] = jnp.exp(s - m).astype(BF16)
            m_ref[b, n * BAND:(n + 1) * BAND, :] = jnp.where(lo, m[0:BAND], m[BAND:2 * BAND])

    def weighted_values(n):
        for b in range(3):
            first = n * BAND + (2 * BAND - keys_of(b))
            pv = _mm(p_ref[n, b, :, 0:keys_of(b)], vp_ref[b, first:first + keys_of(b), :])
            rows = slice(n * BAND, (n + 1) * BAND)
            n_ref[b, rows, :] = jnp.where(lo, pv[0:BAND, 0:LANES], pv[BAND:2 * BAND, 0:LANES])
            l_ref[b, rows, :] = jnp.where(lo, pv[0:BAND, LANES:2 * LANES], pv[BAND:2 * BAND, LANES:2 * LANES])

    groups = nb // ATTN_GROUP
    group = lambda i: range(i * ATTN_GROUP, (i + 1) * ATTN_GROUP) if 0 <= i < groups else ()
    for n in group(0):
        scores(n)
    for i in range(groups + 1):
        for n in group(i - 1):
            weighted_values(n)
        for n in group(i + 1):
            scores(n)
        for n in group(i):
            softmax(n)

    per16 = seq // 16
    for r in range(16):
        sl4 = pl.ds((r % 4) * quarter + r // 4, per16, stride=4)
        blk = pl.ds(r * per16, per16)
        m1, m2 = m_ref[1, sl4, :], m_ref[2, blk, :]
        mx = jnp.maximum(m1, m2)
        e1, e2 = jnp.exp(m1 - mx), jnp.exp(m2 - mx)
        fm_ref[sl4, :] = mx
        fl_ref[sl4, :] = e1 * l_ref[1, sl4, :] + e2 * l_ref[2, blk, :]
        fn_ref[sl4, :] = e1 * n_ref[1, sl4, :] + e2 * n_ref[2, blk, :]
    for r in range(4):
        for c in range(quarter // BAND):
            sl = pl.ds(r + 4 * BAND * c, BAND, stride=4)
            blk = pl.ds(r * quarter + c * BAND, BAND)
            m0, m1 = m_ref[0, sl, :], fm_ref[blk, :]
            mx = jnp.maximum(m0, m1)
            e0, e1 = jnp.exp(m0 - mx), jnp.exp(m1 - mx)
            den = e0 * l_ref[0, sl, :] + e1 * fl_ref[blk, :]
            num = e0 * n_ref[0, sl, :] + e1 * fn_ref[blk, :]
            o_ref[sl, :] = num / den


def _attn_prompt(q, k, v, slopes, batch, seq):
    assert [seq // d // BAND for _, d in DILATED_BRANCHES] == [16, 4, 1]
    nb = seq // BAND
    blk = pl.BlockSpec((None, seq, LANES), lambda p, b: (p, b, 0))
    return pl.pallas_call(
        functools.partial(_attn_prompt_kernel, seq=seq),
        grid=(WIDTH_A // LANES, batch),
        in_specs=[pl.BlockSpec(memory_space=pltpu.SMEM), blk, blk, blk],
        out_specs=blk,
        out_shape=jax.ShapeDtypeStruct((WIDTH_A // LANES, batch * seq, LANES), F32),
        scratch_shapes=[pltpu.VMEM((seq, LANES), F32),
                        pltpu.VMEM((3, nb, 2, BAND, LANES), BF16),
                        pltpu.VMEM((3, BAND + seq, LANES), BF16),
                        pltpu.VMEM((3, BAND + seq, 2 * LANES), BF16),
                        pltpu.VMEM((5, 2 * BAND, 2 * BAND), F32),
                        pltpu.VMEM((nb, 3, 2 * BAND, 2 * BAND), F32),
                        pltpu.VMEM((nb, 3, 2 * BAND, 2 * BAND), BF16)]
        + [pltpu.VMEM((3, seq, LANES), F32)] * 3 + [pltpu.VMEM((seq, LANES), F32)] * 3,
        compiler_params=_params("parallel", "arbitrary"),
        name="attn_prompt",
    )(slopes, q, k, v)


def _attn_sample_kernel(q_ref, kn_ref, vn_ref, kt_ref, vt_ref, o_ref, *, t_new, w_buf, pad):
    rows = N_HEADS_A * t_new
    zeros = jnp.zeros((pad - t_new, WIDTH_A), F32)
    k_new = jnp.concatenate([kn_ref[...], zeros], axis=0).astype(BF16)
    v_new = jnp.concatenate([vn_ref[...], zeros], axis=0).astype(BF16)

    qrep = jnp.concatenate([q_ref[...]] * N_HEADS_A, axis=0)
    rr = lax.broadcasted_iota(jnp.int32, (rows, WIDTH_A), 0)
    ll = lax.broadcasted_iota(jnp.int32, (rows, WIDTH_A), 1)
    qrows = jnp.where(rr // t_new == ll // HEAD_DIM, qrep, 0.0).astype(BF16)
    s = jnp.concatenate([_mm(qrows, kt_ref[...].astype(BF16)), _mm_nt(qrows, k_new)], axis=1)

    r2 = lax.broadcasted_iota(jnp.int32, s.shape, 0)
    u2 = lax.broadcasted_iota(jnp.int32, s.shape, 1)
    d = w_buf + r2 % t_new - u2
    mult = jnp.zeros(s.shape, F32)
    for window, dil in DILATED_BRANCHES:
        mult = mult + jnp.where((d >= 0) & (d <= window) & (d % dil == 0), 1.0, 0.0)
    head = lax.broadcasted_iota(jnp.int32, (rows, 1), 0) // t_new
    slope = jnp.zeros((rows, 1), F32)
    for h in range(N_HEADS_A):
        slope = jnp.where(head == h, 2.0 ** -(h + 1), slope)
    s = jnp.where(mult > 0, s - d.astype(F32) * slope, NEG_BIG)
    m = jnp.max(s, axis=-1, keepdims=True)
    p = mult * jnp.exp(s - m)
    den = jnp.sum(p, axis=-1, keepdims=True)
    p = p.astype(BF16)
    o = (_mm_nt(p[:, 0:w_buf], vt_ref[...].astype(BF16)) + _mm(p[:, w_buf:w_buf + pad], v_new)) * (1.0 / den)

    lane_head = lax.broadcasted_iota(jnp.int32, (t_new, WIDTH_A), 1) // HEAD_DIM
    out = jnp.zeros((t_new, WIDTH_A), F32)
    for h in range(N_HEADS_A):
        out = jnp.where(lane_head == h, o[h * t_new:(h + 1) * t_new, :], out)
    o_ref[...] = out


def _attn_sample(q, k, v, cache_kt, cache_vt, t_new):
    batch, _, w_buf = cache_kt.shape
    assert t_new % SUBLANES == 0 and w_buf >= DILATED_BRANCHES[-1][0] and w_buf % LANES == 0
    pad = LANES
    new = pl.BlockSpec((t_new, WIDTH_A), lambda b: (b, 0))
    cache = pl.BlockSpec((None, WIDTH_A, w_buf), lambda b: (b, 0, 0))
    return pl.pallas_call(
        functools.partial(_attn_sample_kernel, t_new=t_new, w_buf=w_buf, pad=pad),
        grid=(batch,),
        in_specs=[new, new, new, cache, cache],
        out_specs=new,
        out_shape=jax.ShapeDtypeStruct((batch * t_new, WIDTH_A), F32),
        compiler_params=_params("parallel"),
        name="attn_sample",
    )(q, k, v, cache_kt, cache_vt)


def _mem_kv_kernel(mem_ref, g_ref, wk_ref, wv_ref, mk_ref, mv_ref, mkh_ref, mvh_ref):
    h = _rms(mem_ref[...], g_ref[...]).astype(BF16)
    mk = _mm(h, wk_ref[...])
    mv = _mm(h, wv_ref[...])
    mk_ref[...] = mk.astype(BF16)
    mv_ref[...] = mv.astype(BF16)
    mkh_ref[...] = mk.reshape(mkh_ref.shape)
    mvh_ref[...] = mv.reshape(mvh_ref.shape)


def _mem_kv(mem, g, wk, wv, tm):
    n = mem.shape[0]
    row = pl.BlockSpec((tm, D_MODEL), lambda i: (i, 0))
    heads = pl.BlockSpec((tm, N_HEADS_X, HEAD_DIM_X), lambda i: (i, 0, 0))
    return pl.pallas_call(
        _mem_kv_kernel,
        grid=(n // tm,),
        in_specs=[row, _const_spec((1, D_MODEL)), _const_spec(wk.shape), _const_spec(wv.shape)],
        out_specs=[row, row, heads, heads],
        out_shape=[jax.ShapeDtypeStruct((n, D_MODEL), BF16)] * 2
        + [jax.ShapeDtypeStruct((n, N_HEADS_X, HEAD_DIM_X), F32)] * 2,
        compiler_params=_params("parallel"),
        name="mem_kv",
    )(mem, g, wk, wv)


def _shifted_rows(first_row, rows):
    offs = [first_row + j for j in range(CONV_WIDTH) if (first_row + j) % SUBLANES]
    return max(offs) // SUBLANES * SUBLANES + rows


def _realign_conv_input(ext_ref, sh_ref):
    for s in range(1, SUBLANES):
        sh_ref[s - 1] = ext_ref[pl.ds(s, sh_ref.shape[1]), :]


def _conv_module(ext_ref, sh_ref, first_row, row0, rows, cw_ref, cb_ref, lg_ref, lb_ref):
    y = cb_ref[...]
    for j in range(CONV_WIDTH):
        base, s = (first_row + j) // SUBLANES * SUBLANES, (first_row + j) % SUBLANES
        at = pl.ds(base + row0, rows)
        y = y + cw_ref[j:j + 1, :] * (ext_ref[at, :] if s == 0 else sh_ref[s - 1, at, :])
    yc = y - jnp.mean(y, axis=-1, keepdims=True)
    yn = yc * lax.rsqrt(jnp.mean(yc * yc, axis=-1, keepdims=True) + EPS) * lg_ref[...] + lb_ref[...]
    return yn * jax.nn.sigmoid(yn)


def _softmax_rows(s):
    m = jnp.max(s, axis=-1, keepdims=True)
    p = jnp.exp(s - m)
    return p.astype(BF16), 1.0 / jnp.sum(p, axis=-1, keepdims=True)


def _mix_prompt_kernel(x_ref, u_ref, halo_ref, oa_ref, mk_ref, mv_ref, cw_ref, cb_ref, lg_ref, lb_ref,
                       wout_ref, gx_ref, wq_ref, wo_ref, y_ref, ext_ref, sh_ref):
    ts = x_ref.shape[0]
    halo = halo_ref[...]
    ext_ref[0:HALO, :] = jnp.where(pl.program_id(1) == 0, jnp.zeros_like(halo), halo)
    ext_ref[HALO:HALO + ts, :] = u_ref[...]
    _realign_conv_input(ext_ref, sh_ref)
    ob = _conv_module(ext_ref, sh_ref, HALO - (CONV_WIDTH - 1), 0, ts, cw_ref, cb_ref, lg_ref, lb_ref)
    oa = jnp.concatenate([oa_ref[p] for p in range(oa_ref.shape[0])], axis=-1)
    x2 = (x_ref[...] + _mm(oa.astype(BF16), wout_ref[0:WIDTH_A, :])
          + _mm(ob.astype(BF16), wout_ref[WIDTH_A:WIDTH_A + WIDTH_B, :]))
    hx = _rms(x2, gx_ref[...]).astype(BF16)
    qx = (_mm(hx, wq_ref[...]) * (HEAD_DIM_X ** -0.5)).astype(BF16)
    outs = []
    for h in range(N_HEADS_X):
        sl = slice(h * HEAD_DIM_X, (h + 1) * HEAD_DIM_X)
        p, inv = _softmax_rows(_mm_nt(qx[:, sl], mk_ref[:, sl]))
        outs.append((_mm(p, mv_ref[:, sl]) * inv).astype(BF16))
    y_ref[...] = x2 + _mm(jnp.concatenate(outs, axis=-1), wo_ref[...])


def _mix_prompt(x, u, oa, mk, mv, cw, cb, lg, lb, wout, gx, wq, wo, batch, seq, ts):
    tiles = seq // ts
    row = lambda w: pl.BlockSpec((ts, w), lambda b, i: (b * tiles + i, 0))
    oa_spec = pl.BlockSpec((WIDTH_A // LANES, ts, LANES), lambda b, i: (0, b * tiles + i, 0))
    halo = pl.BlockSpec((HALO, WIDTH_B),
                        lambda b, i: (jnp.maximum((b * seq + i * ts) // HALO - 1, 0), 0))
    mem = pl.BlockSpec((N_MEM, D_MODEL), lambda b, i: (b, 0))
    vec = lambda w: _const_spec((1, w))
    return pl.pallas_call(
        _mix_prompt_kernel,
        grid=(batch, tiles),
        in_specs=[row(D_MODEL), row(WIDTH_B), halo, oa_spec, mem, mem, _const_spec(cw.shape),
                  vec(WIDTH_B), vec(WIDTH_B), vec(WIDTH_B), _const_spec(wout.shape), vec(D_MODEL),
                  _const_spec(wq.shape), _const_spec(wo.shape)],
        out_specs=row(D_MODEL),
        out_shape=jax.ShapeDtypeStruct((batch * seq, D_MODEL), F32),
        scratch_shapes=[pltpu.VMEM((HALO + ts, WIDTH_B), F32),
                        pltpu.VMEM((SUBLANES - 1, _shifted_rows(HALO - (CONV_WIDTH - 1), ts), WIDTH_B), F32)],
        compiler_params=_params("parallel", "parallel"),
        name="mix_prompt",
    )(x, u, u, oa, mk, mv, cw, cb, lg, lb, wout, gx, wq, wo)


def _mix_sample_kernel(x_ref, uext_ref, oa_ref, mk_ref, mv_ref, cw_ref, cb_ref, lg_ref, lb_ref,
                       wout_ref, gx_ref, wq_ref, wo_ref, y_ref, sh_ref, *, group, t_new, t_ext):
    rows = group * t_ext
    _realign_conv_input(uext_ref, sh_ref)
    conv = _conv_module(uext_ref, sh_ref, 0, 0, rows, cw_ref, cb_ref, lg_ref, lb_ref)
    ob = conv.reshape(group, t_ext, WIDTH_B)[:, 0:t_new, :].reshape(group * t_new, WIDTH_B)
    x2 = (x_ref[...] + _mm(oa_ref[...].astype(BF16), wout_ref[0:WIDTH_A, :])
          + _mm(ob.astype(BF16), wout_ref[WIDTH_A:WIDTH_A + WIDTH_B, :]))

    hx = _rms(x2, gx_ref[...]).astype(BF16)
    qx = _mm(hx, wq_ref[...]) * (HEAD_DIM_X ** -0.5)
    qrows_n = N_HEADS_X * t_new
    rr = lax.broadcasted_iota(jnp.int32, (qrows_n, D_MODEL), 0)
    ll = lax.broadcasted_iota(jnp.int32, (qrows_n, D_MODEL), 1)
    own = rr // t_new == ll // HEAD_DIM_X
    lane_head = lax.broadcasted_iota(jnp.int32, (t_new, D_MODEL), 1) // HEAD_DIM_X
    outs = []
    for b in range(group):
        qb = qx[b * t_new:(b + 1) * t_new, :]
        qrows = jnp.where(own, jnp.concatenate([qb] * N_HEADS_X, axis=0), 0.0).astype(BF16)
        mk = mk_ref[b].reshape(N_MEM, D_MODEL).astype(BF16)
        mv = mv_ref[b].reshape(N_MEM, D_MODEL).astype(BF16)
        p, inv = _softmax_rows(_mm_nt(qrows, mk))
        o = _mm(p, mv) * inv
        out = jnp.zeros((t_new, D_MODEL), F32)
        for h in range(N_HEADS_X):
            out = jnp.where(lane_head == h, o[h * t_new:(h + 1) * t_new, :], out)
        outs.append(out)
    y_ref[...] = x2 + _mm(jnp.concatenate(outs, axis=0).astype(BF16), wo_ref[...])


def _mix_sample(x, uext, oa, mk, mv, cw, cb, lg, lb, wout, gx, wq, wo, batch, t_new, t_ext, group):
    rows = group * t_new
    row = lambda w: pl.BlockSpec((rows, w), lambda i: (i, 0))
    ext = pl.BlockSpec((group * t_ext + HALO, WIDTH_B), lambda i: (i, 0))
    mem = pl.BlockSpec((group, N_MEM, N_HEADS_X, HEAD_DIM_X), lambda i: (i, 0, 0, 0))
    vec = lambda w: _const_spec((1, w))
    return pl.pallas_call(
        functools.partial(_mix_sample_kernel, group=group, t_new=t_new, t_ext=t_ext),
        grid=(batch // group,),
        in_specs=[row(D_MODEL), ext, row(WIDTH_A), mem, mem, _const_spec(cw.shape),
                  vec(WIDTH_B), vec(WIDTH_B), vec(WIDTH_B), _const_spec(wout.shape), vec(D_MODEL),
                  _const_spec(wq.shape), _const_spec(wo.shape)],
        out_specs=row(D_MODEL),
        out_shape=jax.ShapeDtypeStruct((batch * t_new, D_MODEL), F32),
        scratch_shapes=[pltpu.VMEM((SUBLANES - 1, _shifted_rows(0, group * t_ext), WIDTH_B), F32)],
        compiler_params=_params("parallel"),
        name="mix_sample",
    )(x, uext, oa, mk, mv, cw, cb, lg, lb, wout, gx, wq, wo)


def kernel(x_prompt, x_sample, mem_prompt, cache_win_k, cache_win_v, cache_conv, cache_mem_k, cache_mem_v, ffn1_norm, ffn1_gate, ffn1_up, ffn1_down, mix_norm, w_in, conv_w, conv_b, conv_ln_g, conv_ln_b, w_out, xattn_norm, mem_norm, w_cq, w_ck, w_cv, w_co, ffn2_norm, ffn2_gate, ffn2_up, ffn2_down, final_norm):
    depth = ffn1_norm.shape[0]
    assert depth == 1
    bp, seq, _ = x_prompt.shape
    bs, t_new, _ = x_sample.shape
    keep = CONV_WIDTH - 1
    l = 0
    vec = lambda a: a.reshape(1, -1)
    bf = lambda a: a.astype(BF16)
    slopes = jnp.asarray([2.0 ** -(h + 1) for h in range(N_HEADS_A)], F32)

    f1 = (vec(ffn1_norm[l]), bf(ffn1_gate[l]), bf(ffn1_up[l]), bf(ffn1_down[l]))
    f2 = (vec(ffn2_norm[l]), bf(ffn2_gate[l]), bf(ffn2_up[l]), bf(ffn2_down[l]))
    gmix, win = vec(mix_norm[l]), bf(w_in[l])
    conv = (conv_w[l], vec(conv_b[l]), vec(conv_ln_g[l]), vec(conv_ln_b[l]))
    proj = (bf(w_out[l]), vec(xattn_norm[l]), bf(w_cq[l]), bf(w_co[l]))
    gfin = vec(final_norm)

    xp = x_prompt.reshape(bp * seq, D_MODEL)
    x1, q, k, v, u, kt, vt = _ffn_proj(xp, *f1, gmix, win, tm=512, seq=seq)
    oa = _attn_prompt(q, k, v, slopes, bp, seq)
    mk, mv, mkh, mvh = _mem_kv(mem_prompt.reshape(bp * N_MEM, D_MODEL), vec(mem_norm[l]), bf(w_ck[l]), bf(w_cv[l]),
                               tm=512)
    x3 = _mix_prompt(x1, u, oa, mk, mv, *conv, *proj, batch=bp, seq=seq, ts=1024)
    yp = _ffn_final(x3, *f2, gfin, tm=1024)

    xs = x_sample.reshape(bs * t_new, D_MODEL)
    s1, sq, sk, sv, su, skh, svh = _ffn_proj(xs, *f1, gmix, win, tm=bs * t_new)
    seq_minor = lambda c: jnp.transpose(c, (0, 2, 3, 1)).reshape(c.shape[0], WIDTH_A, c.shape[1])
    soa = _attn_sample(sq, sk, sv, seq_minor(cache_win_k[l]), seq_minor(cache_win_v[l]), t_new)
    u_ext = jnp.concatenate([cache_conv[l], su.reshape(bs, t_new, WIDTH_B)], axis=1)
    t_ext = -(-(keep + t_new) // 8) * 8
    group = 8
    u_pad = jnp.pad(u_ext, ((0, 0), (0, t_ext - keep - t_new), (0, 0))).reshape(bs // group, group * t_ext, WIDTH_B)
    u_pad = jnp.pad(u_pad, ((0, 0), (0, HALO), (0, 0))).reshape(-1, WIDTH_B)
    s3 = _mix_sample(s1, u_pad, soa, cache_mem_k[l], cache_mem_v[l], *conv, *proj,
                     batch=bs, t_new=t_new, t_ext=t_ext, group=group)
    ys = _ffn_final(s3, *f2, gfin, tm=bs * t_new)

    heads = lambda a, b, t: a.reshape(1, b, t, N_HEADS_A, HEAD_DIM)
    from_seq_minor = lambda a: jnp.transpose(a.reshape(1, bp, N_HEADS_A, HEAD_DIM, seq), (0, 1, 4, 2, 3))
    mem_heads = lambda a: a.reshape(1, bp, N_MEM, N_HEADS_X, HEAD_DIM_X)
    return (yp.reshape(bp, seq, D_MODEL),
            ys.reshape(bs, t_new, D_MODEL),
            from_seq_minor(kt), from_seq_minor(vt),
            u.reshape(1, bp, seq, WIDTH_B)[:, :, seq - keep:],
            mem_heads(mkh), mem_heads(mvh),
            heads(skh, bs, t_new), heads(svh, bs, t_new),
            u_ext[None, :, t_new:])
```

```python
import functools

import jax
import jax.numpy as jnp
from jax import lax
from jax.experimental import pallas as pl
from jax.experimental.pallas import tpu as pltpu

D_MODEL = 1024
HEAD_DIM = 64
N_HEADS_A = 8
WIDTH_A = N_HEADS_A * HEAD_DIM
WIDTH_B = D_MODEL - WIDTH_A
DILATED_BRANCHES = ((128, 1), (512, 4), (2048, 16))
BAND = 128
CONV_WIDTH = 31
D_FF = 2816
N_MEM = 256
N_HEADS_X = 4
HEAD_DIM_X = D_MODEL // N_HEADS_X
EPS = 1e-6

LANES = 128
SUBLANES = 8
FF_CHUNK = 256
HALO = 32
ATTN_GROUP = 1
NEG_BIG = -1e30
VMEM_LIMIT = 56 * 1024 * 1024

F32 = jnp.float32
BF16 = jnp.bfloat16


def _const_spec(shape):
    nd = len(shape)
    return pl.BlockSpec(shape, lambda *_: (0,) * nd, pipeline_mode=pl.Buffered(1))


def _params(*sem):
    return pltpu.CompilerParams(dimension_semantics=sem, vmem_limit_bytes=VMEM_LIMIT)


def _rms(x, g):
    return x * lax.rsqrt(jnp.mean(x * x, axis=-1, keepdims=True) + EPS) * g


def _mm(a, b):
    return jnp.dot(a, b, preferred_element_type=F32)


def _mm_nt(a, b):
    return lax.dot_general(a, b, (((1,), (1,)), ((), ())), preferred_element_type=F32)


def _swiglu_half_step(x, g_ref, wg_ref, wu_ref, wd_ref, act_ref):
    h = _rms(x, g_ref[...]).astype(BF16)
    for c in range(0, D_FF, FF_CHUNK):
        gate = _mm(h, wg_ref[:, c:c + FF_CHUNK])
        up = _mm(h, wu_ref[:, c:c + FF_CHUNK])
        act_ref[:, c:c + FF_CHUNK] = (gate * jax.nn.sigmoid(gate) * up).astype(BF16)
    return x + _mm(act_ref[...], wd_ref[...])


def _ffn_proj_kernel(x_ref, g_ref, wg_ref, wu_ref, wd_ref, gmix_ref, win_ref,
                     x1_ref, q_ref, k_ref, v_ref, u_ref, kh_ref, vh_ref, act_ref):
    def put_cols(ref, val):
        if len(ref.shape) == 2:
            ref[...] = val
        else:
            for p in range(ref.shape[0]):
                ref[p] = val[:, p * LANES:(p + 1) * LANES]

    x1 = _swiglu_half_step(x_ref[...], g_ref, wg_ref, wu_ref, wd_ref, act_ref)
    x1_ref[...] = x1
    h = _rms(x1, gmix_ref[...]).astype(BF16)
    w = WIDTH_A
    put_cols(q_ref, _mm(h, win_ref[:, 0:w]) * (HEAD_DIM ** -0.5))
    k = _mm(h, win_ref[:, w:2 * w])
    v = _mm(h, win_ref[:, 2 * w:3 * w])
    put_cols(k_ref, k)
    put_cols(v_ref, v)
    if len(kh_ref.shape) == 2:
        kh_ref[...] = k.T
        vh_ref[...] = v.T
    else:
        kh_ref[...] = k.reshape(kh_ref.shape)
        vh_ref[...] = v.reshape(vh_ref.shape)
    a = _mm(h, win_ref[:, 3 * w:3 * w + WIDTH_B])
    g = _mm(h, win_ref[:, 3 * w + WIDTH_B:3 * w + 2 * WIDTH_B])
    u_ref[...] = a * jax.nn.sigmoid(g)


def _ffn_proj(x, g, wg, wu, wd, gmix, win, tm, seq=None):
    n = x.shape[0]
    row = lambda w: pl.BlockSpec((tm, w), lambda i: (i, 0))
    if seq is None:
        heads = pl.BlockSpec((tm, N_HEADS_A, HEAD_DIM), lambda i: (i, 0, 0))
        heads_shape = jax.ShapeDtypeStruct((n, N_HEADS_A, HEAD_DIM), F32)
        qkv = row(WIDTH_A)
        qkv_shape = jax.ShapeDtypeStruct((n, WIDTH_A), F32)
    else:
        tiles = seq // tm
        heads = pl.BlockSpec((None, WIDTH_A, tm), lambda i: (i // tiles, 0, i % tiles))
        heads_shape = jax.ShapeDtypeStruct((n // seq, WIDTH_A, seq), F32)
        qkv = pl.BlockSpec((WIDTH_A // LANES, tm, LANES), lambda i: (0, i, 0))
        qkv_shape = jax.ShapeDtypeStruct((WIDTH_A // LANES, n, LANES), F32)
    return pl.pallas_call(
        _ffn_proj_kernel,
        grid=(n // tm,),
        in_specs=[row(D_MODEL), _const_spec((1, D_MODEL)), _const_spec(wg.shape), _const_spec(wu.shape),
                  _const_spec(wd.shape), _const_spec((1, D_MODEL)), _const_spec(win.shape)],
        out_specs=[row(D_MODEL), qkv, qkv, qkv, row(WIDTH_B), heads, heads],
        out_shape=[jax.ShapeDtypeStruct((n, D_MODEL), F32)] + [qkv_shape] * 3
        + [jax.ShapeDtypeStruct((n, WIDTH_B), F32)] + [heads_shape] * 2,
        scratch_shapes=[pltpu.VMEM((tm, D_FF), BF16)],
        compiler_params=_params("parallel"),
        name="ffn_proj",
    )(x, g, wg, wu, wd, gmix, win)


def _ffn_final_kernel(x_ref, g_ref, wg_ref, wu_ref, wd_ref, gfin_ref, y_ref, act_ref):
    x1 = _swiglu_half_step(x_ref[...], g_ref, wg_ref, wu_ref, wd_ref, act_ref)
    y_ref[...] = _rms(x1, gfin_ref[...])


def _ffn_final(x, g, wg, wu, wd, gfin, tm):
    n = x.shape[0]
    row = pl.BlockSpec((tm, D_MODEL), lambda i: (i, 0))
    return pl.pallas_call(
        _ffn_final_kernel,
        grid=(n // tm,),
        in_specs=[row, _const_spec((1, D_MODEL)), _const_spec(wg.shape), _const_spec(wu.shape),
                  _const_spec(wd.shape), _const_spec((1, D_MODEL))],
        out_specs=row,
        out_shape=jax.ShapeDtypeStruct((n, D_MODEL), F32),
        scratch_shapes=[pltpu.VMEM((tm, D_FF), BF16)],
        compiler_params=_params("parallel"),
        name="ffn_final",
    )(x, g, wg, wu, wd, gfin)


def _attn_prompt_kernel(slopes_ref, q_ref, k_ref, v_ref, o_ref,
                        x4_ref, qs_ref, kp_ref, vp_ref, bias_ref, s_ref, p_ref, m_ref, l_ref, n_ref,
                        fm_ref, fl_ref, fn_ref, *, seq):
    pair = pl.program_id(0)
    nb = seq // BAND
    lo = lax.broadcasted_iota(jnp.int32, (BAND, LANES), 1) < HEAD_DIM

    @pl.when(pl.program_id(1) == 0)
    def _():
        kp_ref[:, 0:BAND, :] = jnp.zeros((3, BAND, LANES), BF16)
        vp_ref[:, 0:BAND, 0:LANES] = jnp.zeros((3, BAND, LANES), BF16)
        vp_ref[:, :, LANES:2 * LANES] = jnp.ones((3, BAND + seq, LANES), BF16)
        qi = lax.broadcasted_iota(jnp.int32, (BAND, 2 * BAND), 0)
        kj = lax.broadcasted_iota(jnp.int32, (BAND, 2 * BAND), 1)
        dist = qi + BAND - kj
        neg_dist = jnp.where((dist >= 0) & (dist <= BAND), -dist.astype(F32), NEG_BIG)
        neg_dist_cur = jnp.where(kj >= BAND, neg_dist, NEG_BIG)
        for b, (_, dil) in enumerate(DILATED_BRANCHES):
            for noprev in range(2):
                if 2 * b + noprev < bias_ref.shape[0]:
                    table = neg_dist_cur if noprev else neg_dist
                    for half in range(2):
                        bias_ref[2 * b + noprev, half * BAND:(half + 1) * BAND, :] = (
                            (table * slopes_ref[2 * pair + half]) * float(dil))

    def put(kind, b, first, x):
        nblocks = x.shape[0] // BAND
        if kind == 0:
            is_lo = lax.broadcasted_iota(jnp.int32, x.shape, 1) < HEAD_DIM
            x_lo = jnp.where(is_lo, x, 0.0).astype(BF16)
            x_hi = jnp.where(is_lo, 0.0, x).astype(BF16)
            for j in range(nblocks):
                qs_ref[b, first + j, 0] = x_lo[j * BAND:(j + 1) * BAND]
                qs_ref[b, first + j, 1] = x_hi[j * BAND:(j + 1) * BAND]
        elif kind == 1:
            kp_ref[b, BAND + first * BAND:BAND + first * BAND + x.shape[0], :] = x.astype(BF16)
        else:
            vp_ref[b, BAND + first * BAND:BAND + first * BAND + x.shape[0], 0:LANES] = x.astype(BF16)

    quarter = seq // 4
    for kind, ref in enumerate((q_ref, k_ref, v_ref)):
        for r in range(4):
            put(kind, 0, 4 * r, ref[r * quarter:(r + 1) * quarter, :])
            x = ref[pl.ds(r, quarter, stride=4), :]
            x4_ref[r * quarter:(r + 1) * quarter, :] = x
            put(kind, 1, 4 * r, x)
        for r in range(16):
            put(kind, 2, r, x4_ref[pl.ds((r % 4) * quarter + r // 4, seq // 16, stride=4), :])

    def blocks_per_class(b):
        return seq // DILATED_BRANCHES[b][1] // BAND

    def keys_of(b):
        return 2 * BAND if blocks_per_class(b) > 1 else BAND

    def scores(n):
        for b in range(3):
            qb = qs_ref[b, n].reshape(2 * BAND, LANES)
            first = n * BAND + (2 * BAND - keys_of(b))
            s_ref[n, b, :, 0:keys_of(b)] = _mm_nt(qb, kp_ref[b, first:first + keys_of(b), :])

    def softmax(n):
        for b in range(3):
            if blocks_per_class(b) > 1:
                bias = bias_ref[2 * b + int(n % blocks_per_class(b) == 0)]
            else:
                bias = bias_ref[2 * b, :, BAND:2 * BAND]
            s = s_ref[n, b, :, 0:keys_of(b)] + bias
            m = jnp.max(s, axis=-1, keepdims=True)
            p_ref[n, b, :, 0:keys_of(b)] = jnp.exp(s - m).astype(BF16)
            m_ref[b, n * BAND:(n + 1) * BAND, :] = jnp.where(lo, m[0:BAND], m[BAND:2 * BAND])

    def weighted_values(n):
        for b in range(3):
            first = n * BAND + (2 * BAND - keys_of(b))
            pv = _mm(p_ref[n, b, :, 0:keys_of(b)], vp_ref[b, first:first + keys_of(b), :])
            rows = slice(n * BAND, (n + 1) * BAND)
            n_ref[b, rows, :] = jnp.where(lo, pv[0:BAND, 0:LANES], pv[BAND:2 * BAND, 0:LANES])
            l_ref[b, rows, :] = jnp.where(lo, pv[0:BAND, LANES:2 * LANES], pv[BAND:2 * BAND, LANES:2 * LANES])

    groups = nb // ATTN_GROUP
    group = lambda i: range(i * ATTN_GROUP, (i + 1) * ATTN_GROUP) if 0 <= i < groups else ()
    for n in group(0):
        scores(n)
    for i in range(groups + 1):
        for n in group(i - 1):
            weighted_values(n)
        for n in group(i + 1):
            scores(n)
        for n in group(i):
            softmax(n)

    per16 = seq // 16
    for r in range(16):
        sl4 = pl.ds((r % 4) * quarter + r // 4, per16, stride=4)
        blk = pl.ds(r * per16, per16)
        m1, m2 = m_ref[1, sl4, :], m_ref[2, blk, :]
        mx = jnp.maximum(m1, m2)
        e1, e2 = jnp.exp(m1 - mx), jnp.exp(m2 - mx)
        fm_ref[sl4, :] = mx
        fl_ref[sl4, :] = e1 * l_ref[1, sl4, :] + e2 * l_ref[2, blk, :]
        fn_ref[sl4, :] = e1 * n_ref[1, sl4, :] + e2 * n_ref[2, blk, :]
    for r in range(4):
        for c in range(quarter // BAND):
            sl = pl.ds(r + 4 * BAND * c, BAND, stride=4)
            blk = pl.ds(r * quarter + c * BAND, BAND)
            m0, m1 = m_ref[0, sl, :], fm_ref[blk, :]
            mx = jnp.maximum(m0, m1)
            e0, e1 = jnp.exp(m0 - mx), jnp.exp(m1 - mx)
            den = e0 * l_ref[0, sl, :] + e1 * fl_ref[blk, :]
            num = e0 * n_ref[0, sl, :] + e1 * fn_ref[blk, :]
            o_ref[sl, :] = num / den


def _attn_prompt(q, k, v, slopes, batch, seq):
    assert [seq // d // BAND for _, d in DILATED_BRANCHES] == [16, 4, 1]
    nb = seq // BAND
    blk = pl.BlockSpec((None, seq, LANES), lambda p, b: (p, b, 0))
    return pl.pallas_call(
        functools.partial(_attn_prompt_kernel, seq=seq),
        grid=(WIDTH_A // LANES, batch),
        in_specs=[pl.BlockSpec(memory_space=pltpu.SMEM), blk, blk, blk],
        out_specs=blk,
        out_shape=jax.ShapeDtypeStruct((WIDTH_A // LANES, batch * seq, LANES), F32),
        scratch_shapes=[pltpu.VMEM((seq, LANES), F32),
                        pltpu.VMEM((3, nb, 2, BAND, LANES), BF16),
                        pltpu.VMEM((3, BAND + seq, LANES), BF16),
                        pltpu.VMEM((3, BAND + seq, 2 * LANES), BF16),
                        pltpu.VMEM((5, 2 * BAND, 2 * BAND), F32),
                        pltpu.VMEM((nb, 3, 2 * BAND, 2 * BAND), F32),
                        pltpu.VMEM((nb, 3, 2 * BAND, 2 * BAND), BF16)]
        + [pltpu.VMEM((3, seq, LANES), F32)] * 3 + [pltpu.VMEM((seq, LANES), F32)] * 3,
        compiler_params=_params("parallel", "arbitrary"),
        name="attn_prompt",
    )(slopes, q, k, v)


def _attn_sample_kernel(q_ref, kn_ref, vn_ref, kt_ref, vt_ref, o_ref, *, t_new, w_buf, pad):
    rows = N_HEADS_A * t_new
    zeros = jnp.zeros((pad - t_new, WIDTH_A), F32)
    k_new = jnp.concatenate([kn_ref[...], zeros], axis=0).astype(BF16)
    v_new = jnp.concatenate([vn_ref[...], zeros], axis=0).astype(BF16)

    qrep = jnp.concatenate([q_ref[...]] * N_HEADS_A, axis=0)
    rr = lax.broadcasted_iota(jnp.int32, (rows, WIDTH_A), 0)
    ll = lax.broadcasted_iota(jnp.int32, (rows, WIDTH_A), 1)
    qrows = jnp.where(rr // t_new == ll // HEAD_DIM, qrep, 0.0).astype(BF16)
    s = jnp.concatenate([_mm(qrows, kt_ref[...].astype(BF16)), _mm_nt(qrows, k_new)], axis=1)

    r2 = lax.broadcasted_iota(jnp.int32, s.shape, 0)
    u2 = lax.broadcasted_iota(jnp.int32, s.shape, 1)
    d = w_buf + r2 % t_new - u2
    mult = jnp.zeros(s.shape, F32)
    for window, dil in DILATED_BRANCHES:
        mult = mult + jnp.where((d >= 0) & (d <= window) & (d % dil == 0), 1.0, 0.0)
    head = lax.broadcasted_iota(jnp.int32, (rows, 1), 0) // t_new
    slope = jnp.zeros((rows, 1), F32)
    for h in range(N_HEADS_A):
        slope = jnp.where(head == h, 2.0 ** -(h + 1), slope)
    s = jnp.where(mult > 0, s - d.astype(F32) * slope, NEG_BIG)
    m = jnp.max(s, axis=-1, keepdims=True)
    p = mult * jnp.exp(s - m)
    den = jnp.sum(p, axis=-1, keepdims=True)
    p = p.astype(BF16)
    o = (_mm_nt(p[:, 0:w_buf], vt_ref[...].astype(BF16)) + _mm(p[:, w_buf:w_buf + pad], v_new)) * (1.0 / den)

    lane_head = lax.broadcasted_iota(jnp.int32, (t_new, WIDTH_A), 1) // HEAD_DIM
    out = jnp.zeros((t_new, WIDTH_A), F32)
    for h in range(N_HEADS_A):
        out = jnp.where(lane_head == h, o[h * t_new:(h + 1) * t_new, :], out)
    o_ref[...] = out


def _attn_sample(q, k, v, cache_kt, cache_vt, t_new):
    batch, _, w_buf = cache_kt.shape
    assert t_new % SUBLANES == 0 and w_buf >= DILATED_BRANCHES[-1][0] and w_buf % LANES == 0
    pad = LANES
    new = pl.BlockSpec((t_new, WIDTH_A), lambda b: (b, 0))
    cache = pl.BlockSpec((None, WIDTH_A, w_buf), lambda b: (b, 0, 0))
    return pl.pallas_call(
        functools.partial(_attn_sample_kernel, t_new=t_new, w_buf=w_buf, pad=pad),
        grid=(batch,),
        in_specs=[new, new, new, cache, cache],
        out_specs=new,
        out_shape=jax.ShapeDtypeStruct((batch * t_new, WIDTH_A), F32),
        compiler_params=_params("parallel"),
        name="attn_sample",
    )(q, k, v, cache_kt, cache_vt)


def _mem_kv_kernel(mem_ref, g_ref, wk_ref, wv_ref, mk_ref, mv_ref, mkh_ref, mvh_ref):
    h = _rms(mem_ref[...], g_ref[...]).astype(BF16)
    mk = _mm(h, wk_ref[...])
    mv = _mm(h, wv_ref[...])
    mk_ref[...] = mk.astype(BF16)
    mv_ref[...] = mv.astype(BF16)
    mkh_ref[...] = mk.reshape(mkh_ref.shape)
    mvh_ref[...] = mv.reshape(mvh_ref.shape)


def _mem_kv(mem, g, wk, wv, tm):
    n = mem.shape[0]
    row = pl.BlockSpec((tm, D_MODEL), lambda i: (i, 0))
    heads = pl.BlockSpec((tm, N_HEADS_X, HEAD_DIM_X), lambda i: (i, 0, 0))
    return pl.pallas_call(
        _mem_kv_kernel,
        grid=(n // tm,),
        in_specs=[row, _const_spec((1, D_MODEL)), _const_spec(wk.shape), _const_spec(wv.shape)],
        out_specs=[row, row, heads, heads],
        out_shape=[jax.ShapeDtypeStruct((n, D_MODEL), BF16)] * 2
        + [jax.ShapeDtypeStruct((n, N_HEADS_X, HEAD_DIM_X), F32)] * 2,
        compiler_params=_params("parallel"),
        name="mem_kv",
    )(mem, g, wk, wv)


def _shifted_rows(first_row, rows):
    offs = [first_row + j for j in range(CONV_WIDTH) if (first_row + j) % SUBLANES]
    return max(offs) // SUBLANES * SUBLANES + rows


def _realign_conv_input(ext_ref, sh_ref):
    for s in range(1, SUBLANES):
        sh_ref[s - 1] = ext_ref[pl.ds(s, sh_ref.shape[1]), :]


def _conv_module(ext_ref, sh_ref, first_row, row0, rows, cw_ref, cb_ref, lg_ref, lb_ref):
    y = cb_ref[...]
    for j in range(CONV_WIDTH):
        base, s = (first_row + j) // SUBLANES * SUBLANES, (first_row + j) % SUBLANES
        at = pl.ds(base + row0, rows)
        y = y + cw_ref[j:j + 1, :] * (ext_ref[at, :] if s == 0 else sh_ref[s - 1, at, :])
    yc = y - jnp.mean(y, axis=-1, keepdims=True)
    yn = yc * lax.rsqrt(jnp.mean(yc * yc, axis=-1, keepdims=True) + EPS) * lg_ref[...] + lb_ref[...]
    return yn * jax.nn.sigmoid(yn)


def _softmax_rows(s):
    m = jnp.max(s, axis=-1, keepdims=True)
    p = jnp.exp(s - m)
    return p.astype(BF16), 1.0 / jnp.sum(p, axis=-1, keepdims=True)


def _mix_prompt_kernel(x_ref, u_ref, halo_ref, oa_ref, mk_ref, mv_ref, cw_ref, cb_ref, lg_ref, lb_ref,
                       wout_ref, gx_ref, wq_ref, wo_ref, y_ref, ext_ref, sh_ref):
    ts = x_ref.shape[0]
    halo = halo_ref[...]
    ext_ref[0:HALO, :] = jnp.where(pl.program_id(1) == 0, jnp.zeros_like(halo), halo)
    ext_ref[HALO:HALO + ts, :] = u_ref[...]
    _realign_conv_input(ext_ref, sh_ref)
    ob = _conv_module(ext_ref, sh_ref, HALO - (CONV_WIDTH - 1), 0, ts, cw_ref, cb_ref, lg_ref, lb_ref)
    oa = jnp.concatenate([oa_ref[p] for p in range(oa_ref.shape[0])], axis=-1)
    x2 = (x_ref[...] + _mm(oa.astype(BF16), wout_ref[0:WIDTH_A, :])
          + _mm(ob.astype(BF16), wout_ref[WIDTH_A:WIDTH_A + WIDTH_B, :]))
    hx = _rms(x2, gx_ref[...]).astype(BF16)
    qx = (_mm(hx, wq_ref[...]) * (HEAD_DIM_X ** -0.5)).astype(BF16)
    outs = []
    for h in range(N_HEADS_X):
        sl = slice(h * HEAD_DIM_X, (h + 1) * HEAD_DIM_X)
        p, inv = _softmax_rows(_mm_nt(qx[:, sl], mk_ref[:, sl]))
        outs.append((_mm(p, mv_ref[:, sl]) * inv).astype(BF16))
    y_ref[...] = x2 + _mm(jnp.concatenate(outs, axis=-1), wo_ref[...])


def _mix_prompt(x, u, oa, mk, mv, cw, cb, lg, lb, wout, gx, wq, wo, batch, seq, ts):
    tiles = seq // ts
    row = lambda w: pl.BlockSpec((ts, w), lambda b, i: (b * tiles + i, 0))
    oa_spec = pl.BlockSpec((WIDTH_A // LANES, ts, LANES), lambda b, i: (0, b * tiles + i, 0))
    halo = pl.BlockSpec((HALO, WIDTH_B),
                        lambda b, i: (jnp.maximum((b * seq + i * ts) // HALO - 1, 0), 0))
    mem = pl.BlockSpec((N_MEM, D_MODEL), lambda b, i: (b, 0))
    vec = lambda w: _const_spec((1, w))
    return pl.pallas_call(
        _mix_prompt_kernel,
        grid=(batch, tiles),
        in_specs=[row(D_MODEL), row(WIDTH_B), halo, oa_spec, mem, mem, _const_spec(cw.shape),
                  vec(WIDTH_B), vec(WIDTH_B), vec(WIDTH_B), _const_spec(wout.shape), vec(D_MODEL),
                  _const_spec(wq.shape), _const_spec(wo.shape)],
        out_specs=row(D_MODEL),
        out_shape=jax.ShapeDtypeStruct((batch * seq, D_MODEL), F32),
        scratch_shapes=[pltpu.VMEM((HALO + ts, WIDTH_B), F32),
                        pltpu.VMEM((SUBLANES - 1, _shifted_rows(HALO - (CONV_WIDTH - 1), ts), WIDTH_B), F32)],
        compiler_params=_params("parallel", "parallel"),
        name="mix_prompt",
    )(x, u, u, oa, mk, mv, cw, cb, lg, lb, wout, gx, wq, wo)


def _mix_sample_kernel(x_ref, uext_ref, oa_ref, mk_ref, mv_ref, cw_ref, cb_ref, lg_ref, lb_ref,
                       wout_ref, gx_ref, wq_ref, wo_ref, y_ref, sh_ref, *, group, t_new, t_ext):
    rows = group * t_ext
    _realign_conv_input(uext_ref, sh_ref)
    conv = _conv_module(uext_ref, sh_ref, 0, 0, rows, cw_ref, cb_ref, lg_ref, lb_ref)
    ob = conv.reshape(group, t_ext, WIDTH_B)[:, 0:t_new, :].reshape(group * t_new, WIDTH_B)
    x2 = (x_ref[...] + _mm(oa_ref[...].astype(BF16), wout_ref[0:WIDTH_A, :])
          + _mm(ob.astype(BF16), wout_ref[WIDTH_A:WIDTH_A + WIDTH_B, :]))

    hx = _rms(x2, gx_ref[...]).astype(BF16)
    qx = _mm(hx, wq_ref[...]) * (HEAD_DIM_X ** -0.5)
    qrows_n = N_HEADS_X * t_new
    rr = lax.broadcasted_iota(jnp.int32, (qrows_n, D_MODEL), 0)
    ll = lax.broadcasted_iota(jnp.int32, (qrows_n, D_MODEL), 1)
    own = rr // t_new == ll // HEAD_DIM_X
    lane_head = lax.broadcasted_iota(jnp.int32, (t_new, D_MODEL), 1) // HEAD_DIM_X
    outs = []
    for b in range(group):
        qb = qx[b * t_new:(b + 1) * t_new, :]
        qrows = jnp.where(own, jnp.concatenate([qb] * N_HEADS_X, axis=0), 0.0).astype(BF16)
        mk = mk_ref[b].reshape(N_MEM, D_MODEL).astype(BF16)
        mv = mv_ref[b].reshape(N_MEM, D_MODEL).astype(BF16)
        p, inv = _softmax_rows(_mm_nt(qrows, mk))
        o = _mm(p, mv) * inv
        out = jnp.zeros((t_new, D_MODEL), F32)
        for h in range(N_HEADS_X):
            out = jnp.where(lane_head == h, o[h * t_new:(h + 1) * t_new, :], out)
        outs.append(out)
    y_ref[...] = x2 + _mm(jnp.concatenate(outs, axis=0).astype(BF16), wo_ref[...])


def _mix_sample(x, uext, oa, mk, mv, cw, cb, lg, lb, wout, gx, wq, wo, batch, t_new, t_ext, group):
    rows = group * t_new
    row = lambda w: pl.BlockSpec((rows, w), lambda i: (i, 0))
    ext = pl.BlockSpec((group * t_ext + HALO, WIDTH_B), lambda i: (i, 0))
    mem = pl.BlockSpec((group, N_MEM, N_HEADS_X, HEAD_DIM_X), lambda i: (i, 0, 0, 0))
    vec = lambda w: _const_spec((1, w))
    return pl.pallas_call(
        functools.partial(_mix_sample_kernel, group=group, t_new=t_new, t_ext=t_ext),
        grid=(batch // group,),
        in_specs=[row(D_MODEL), ext, row(WIDTH_A), mem, mem, _const_spec(cw.shape),
                  vec(WIDTH_B), vec(WIDTH_B), vec(WIDTH_B), _const_spec(wout.shape), vec(D_MODEL),
                  _const_spec(wq.shape), _const_spec(wo.shape)],
        out_specs=row(D_MODEL),
        out_shape=jax.ShapeDtypeStruct((batch * t_new, D_MODEL), F32),
        scratch_shapes=[pltpu.VMEM((SUBLANES - 1, _shifted_rows(0, group * t_ext), WIDTH_B), F32)],
        compiler_params=_params("parallel"),
        name="mix_sample",
    )(x, uext, oa, mk, mv, cw, cb, lg, lb, wout, gx, wq, wo)


def kernel(x_prompt, x_sample, mem_prompt, cache_win_k, cache_win_v, cache_conv, cache_mem_k, cache_mem_v, ffn1_norm, ffn1_gate, ffn1_up, ffn1_down, mix_norm, w_in, conv_w, conv_b, conv_ln_g, conv_ln_b, w_out, xattn_norm, mem_norm, w_cq, w_ck, w_cv, w_co, ffn2_norm, ffn2_gate, ffn2_up, ffn2_down, final_norm):
    depth = ffn1_norm.shape[0]
    assert depth == 1
    bp, seq, _ = x_prompt.shape
    bs, t_new, _ = x_sample.shape
    keep = CONV_WIDTH - 1
    l = 0
    vec = lambda a: a.reshape(1, -1)
    bf = lambda a: a.astype(BF16)
    slopes = jnp.asarray([2.0 ** -(h + 1) for h in range(N_HEADS_A)], F32)

    f1 = (vec(ffn1_norm[l]), bf(ffn1_gate[l]), bf(ffn1_up[l]), bf(0.5 * ffn1_down[l]))
    f2 = (vec(ffn2_norm[l]), bf(ffn2_gate[l]), bf(ffn2_up[l]), bf(0.5 * ffn2_down[l]))
    gmix, win = vec(mix_norm[l]), bf(w_in[l])
    conv = (conv_w[l], vec(conv_b[l]), vec(conv_ln_g[l]), vec(conv_ln_b[l]))
    proj = (bf(w_out[l]), vec(xattn_norm[l]), bf(w_cq[l]), bf(w_co[l]))
    gfin = vec(final_norm)

    xp = x_prompt.reshape(bp * seq, D_MODEL)
    x1, q, k, v, u, kt, vt = _ffn_proj(xp, *f1, gmix, win, tm=512, seq=seq)
    oa = _attn_prompt(q, k, v, slopes, bp, seq)
    mk, mv, mkh, mvh = _mem_kv(mem_prompt.reshape(bp * N_MEM, D_MODEL), vec(mem_norm[l]), bf(w_ck[l]), bf(w_cv[l]),
                               tm=512)
    x3 = _mix_prompt(x1, u, oa, mk, mv, *conv, *proj, batch=bp, seq=seq, ts=1024)
    yp = _ffn_final(x3, *f2, gfin, tm=1024)

    xs = x_sample.reshape(bs * t_new, D_MODEL)
    s1, sq, sk, sv, su, skh, svh = _ffn_proj(xs, *f1, gmix, win, tm=bs * t_new)
    seq_minor = lambda c: jnp.transpose(c, (0, 2, 3, 1)).reshape(c.shape[0], WIDTH_A, c.shape[1])
    soa = _attn_sample(sq, sk, sv, seq_minor(cache_win_k[l]), seq_minor(cache_win_v[l]), t_new)
    u_ext = jnp.concatenate([cache_conv[l], su.reshape(bs, t_new, WIDTH_B)], axis=1)
    t_ext = -(-(keep + t_new) // 8) * 8
    group = 8
    u_pad = jnp.pad(u_ext, ((0, 0), (0, t_ext - keep - t_new), (0, 0))).reshape(bs // group, group * t_ext, WIDTH_B)
    u_pad = jnp.pad(u_pad, ((0, 0), (0, HALO), (0, 0))).reshape(-1, WIDTH_B)
    s3 = _mix_sample(s1, u_pad, soa, cache_mem_k[l], cache_mem_v[l], *conv, *proj,
                     batch=bs, t_new=t_new, t_ext=t_ext, group=group)
    ys = _ffn_final(s3, *f2, gfin, tm=bs * t_new)

    heads = lambda a, b, t: a.reshape(1, b, t, N_HEADS_A, HEAD_DIM)
    from_seq_minor = lambda a: jnp.transpose(a.reshape(1, bp, N_HEADS_A, HEAD_DIM, seq), (0, 1, 4, 2, 3))
    mem_heads = lambda a: a.reshape(1, bp, N_MEM, N_HEADS_X, HEAD_DIM_X)
    return (yp.reshape(bp, seq, D_MODEL),
            ys.reshape(bs, t_new, D_MODEL),
            from_seq_minor(kt), from_seq_minor(vt),
            u.reshape(1, bp, seq, WIDTH_B)[:, :, seq - keep:],
            mem_heads(mkh), mem_heads(mvh),
            heads(skh, bs, t_new), heads(svh, bs, t_new),
            u_ext[None, :, t_new:])
```
